```python
import jax
import jax.numpy as jnp
from jax import lax
import numpy as np

D_MODEL = 1024
BATCH = 4
SEQ = 4096
DEPTH = 1
DEC_BATCH = 128
DEC_SEQ = 1
PAST_LEN = 8192
PAGE_SIZE = 128

POOL_WINDOWS = (2, 4, 8, 16)
POOL_GROUPS = 4
POOL_GROUP_WIDTH = D_MODEL // 8
POOL_WIDTH = POOL_GROUPS * POOL_GROUP_WIDTH
POOL_BUF = 15
HEAD_DIM = 64
N_HEADS = D_MODEL // 128
ATTN_WIDTH = N_HEADS * HEAD_DIM
ROT_DIM = HEAD_DIM // 4
ROPE_THETA = 500000.0
MOBA_BLOCK = 256
MOBA_TOPK = 3
Q_CHUNK = 32
N_GROUPS = 4
EXPERTS_PER_GROUP = 8
EXPERT_TOPK = 2
EXPERT_FF = D_MODEL // 4
ALPHA = (2 * DEPTH) ** 0.25
BETA = (8 * DEPTH) ** -0.25
LN_EPS = 1e-5
NEG_INF = -1e30
IN_WIDTH = POOL_WIDTH + 3 * ATTN_WIDTH + 2 * D_MODEL

kernel_name = 'pool_moba_hmoe_deepnorm_step'


def layer_norm(x, g, b):
    xf = x.astype(jnp.float32)
    mu = jnp.mean(xf, axis=-1, keepdims=True)
    var = jnp.mean(jnp.square(xf - mu), axis=-1, keepdims=True)
    return ((xf - mu) * lax.rsqrt(var + LN_EPS) * g + b).astype(x.dtype)


def rotary(x, pos):
    half = ROT_DIM // 2
    inv = ROPE_THETA ** (-2.0 * jnp.arange(half, dtype=jnp.float32) / ROT_DIM)
    ang = pos[:, None] * inv[None, :]
    cos = jnp.cos(ang)[None, :, None, :]
    sin = jnp.sin(ang)[None, :, None, :]
    xr = x[..., :ROT_DIM].astype(jnp.float32)
    x1, x2 = xr[..., :half], xr[..., half:]
    rot = jnp.concatenate([x1 * cos - x2 * sin, x2 * cos + x1 * sin], axis=-1).astype(x.dtype)
    return jnp.concatenate([rot, x[..., ROT_DIM:]], axis=-1)


def project(x, w_in, start):
    n, L, _ = x.shape
    z = jnp.einsum('nld,de->nle', x, w_in)
    a0 = POOL_WIDTH
    u = z[..., :a0]
    q = z[..., a0:a0 + ATTN_WIDTH].reshape(n, L, N_HEADS, HEAD_DIM)
    k = z[..., a0 + ATTN_WIDTH:a0 + 2 * ATTN_WIDTH].reshape(n, L, N_HEADS, HEAD_DIM)
    v = z[..., a0 + 2 * ATTN_WIDTH:a0 + 3 * ATTN_WIDTH].reshape(n, L, N_HEADS, HEAD_DIM)
    gates = jax.nn.sigmoid(z[..., a0 + 3 * ATTN_WIDTH:].astype(jnp.float32)).astype(x.dtype)
    pos = start + jnp.arange(L, dtype=jnp.float32)
    return u, rotary(q, pos), rotary(k, pos), v, gates[..., :D_MODEL], gates[..., D_MODEL:]


def pool_mix(u, prefix, start, w_group, scale):
    n, L, _ = u.shape
    ext = jnp.concatenate([prefix.astype(u.dtype), u], axis=1)
    cs = jnp.cumsum(ext.astype(jnp.float32), axis=1)
    cs = jnp.concatenate([jnp.zeros((n, 1, POOL_WIDTH), jnp.float32), cs], axis=1)
    pos = start + jnp.arange(L)
    parts = []
    for g, w in enumerate(POOL_WINDOWS):
        c0, c1 = g * POOL_GROUP_WIDTH, (g + 1) * POOL_GROUP_WIDTH
        s = cs[:, POOL_BUF + 1:POOL_BUF + 1 + L, c0:c1] - cs[:, POOL_BUF + 1 - w:POOL_BUF + 1 - w + L, c0:c1]
        cnt = jnp.minimum(pos + 1, w).astype(jnp.float32)
        parts.append(s / cnt[None, :, None])
    pooled = (jnp.concatenate(parts, axis=-1) - u.astype(jnp.float32)).astype(u.dtype)
    y = jnp.einsum('nlgc,gcd->nlgd', pooled.reshape(n, L, POOL_GROUPS, POOL_GROUP_WIDTH), w_group)
    y = y.reshape(n, L, POOL_WIDTH) * scale
    return y, ext[:, -POOL_BUF:]


def moba_prompt(q, k, v):
    n, S, H, dh = q.shape
    nb = -(-S // MOBA_BLOCK)
    pad = nb * MOBA_BLOCK - S
    qh = jnp.swapaxes(q, 1, 2) * (HEAD_DIM ** -0.5)
    kh = jnp.pad(jnp.swapaxes(k, 1, 2), ((0, 0), (0, 0), (0, pad), (0, 0)))
    vh = jnp.pad(jnp.swapaxes(v, 1, 2), ((0, 0), (0, 0), (0, pad), (0, 0)))
    kb = kh.reshape(n, H, nb, MOBA_BLOCK, dh)
    vb = vh.reshape(n, H, nb, MOBA_BLOCK, dh)
    kmean = jnp.mean(kb, axis=3, dtype=jnp.float32)
    k_eff = min(MOBA_TOPK, nb - 1)
    bi = jnp.arange(n)[:, None, None, None]
    hi = jnp.arange(H)[None, :, None, None]

    def chunk(c):
        q0 = c * Q_CHUNK
        blk = q0 // MOBA_BLOCK
        qc = lax.dynamic_slice_in_dim(qh, q0, Q_CHUNK, axis=2)
        qpos = q0 + jnp.arange(Q_CHUNK)
        k_own = lax.dynamic_slice_in_dim(kh, blk * MOBA_BLOCK, MOBA_BLOCK, axis=2)
        v_own = lax.dynamic_slice_in_dim(vh, blk * MOBA_BLOCK, MOBA_BLOCK, axis=2)
        kpos = blk * MOBA_BLOCK + jnp.arange(MOBA_BLOCK)
        l_own = jnp.einsum('nhqd,nhjd->nhqj', qc, k_own).astype(jnp.float32)
        l_own = jnp.where(kpos[None, :] <= qpos[:, None], l_own, NEG_INF)
        if k_eff == 0:
            p = jax.nn.softmax(l_own, axis=-1).astype(v.dtype)
            return jnp.einsum('nhqj,nhjd->nhqd', p, v_own)
        gate = jnp.einsum('nhqd,nhbd->nhqb', qc.astype(jnp.float32), kmean)
        gate = jnp.where(jnp.arange(nb) < blk, gate, NEG_INF)
        _, idx = lax.top_k(gate, k_eff)
        k_sel = kb[bi, hi, idx]
        v_sel = vb[bi, hi, idx]
        l_sel = jnp.einsum('nhqd,nhqkjd->nhqkj', qc, k_sel).astype(jnp.float32)
        l_sel = jnp.where((jnp.arange(k_eff) < blk)[:, None], l_sel, NEG_INF)
        logits = jnp.concatenate([l_sel.reshape(n, H, Q_CHUNK, k_eff * MOBA_BLOCK), l_own], axis=-1)
        p = jax.nn.softmax(logits, axis=-1).astype(v.dtype)
        p_sel = p[..., :k_eff * MOBA_BLOCK].reshape(n, H, Q_CHUNK, k_eff, MOBA_BLOCK)
        p_own = p[..., k_eff * MOBA_BLOCK:]
        return (jnp.einsum('nhqkj,nhqkjd->nhqd', p_sel, v_sel)
                + jnp.einsum('nhqj,nhjd->nhqd', p_own, v_own))

    out = lax.map(chunk, jnp.arange(S // Q_CHUNK))
    return out.transpose(1, 0, 3, 2, 4).reshape(n, S, H * dh)


def moba_sample(q, k_new, v_new, cache_k, cache_v, page_means, page_table, layer):
    n, L, H, dh = q.shape
    ppb = MOBA_BLOCK // PAGE_SIZE
    past_len = page_table.shape[1] * PAGE_SIZE
    n_past = past_len // MOBA_BLOCK
    r = past_len - n_past * MOBA_BLOCK
    qs = q * (HEAD_DIM ** -0.5)
    own_pages = page_table[:, n_past * ppb:]
    k_own = jnp.concatenate([cache_k[layer, own_pages].reshape(n, r, H, dh), k_new], axis=1)
    v_own = jnp.concatenate([cache_v[layer, own_pages].reshape(n, r, H, dh), v_new], axis=1)
    l_own = jnp.einsum('nthd,njhd->nthj', qs, k_own).astype(jnp.float32)
    own_mask = jnp.arange(r + L)[None, :] <= (r + jnp.arange(L))[:, None]
    l_own = jnp.where(own_mask[None, :, None, :], l_own, NEG_INF)
    k_eff = min(MOBA_TOPK, n_past)
    if k_eff == 0:
        p = jax.nn.softmax(l_own, axis=-1).astype(v_new.dtype)
        return jnp.einsum('nthj,njhd->nthd', p, v_own).reshape(n, L, H * dh)
    blk_pages = page_table[:, :n_past * ppb].reshape(n, n_past, ppb)
    bmean = jnp.mean(page_means[layer, blk_pages], axis=2)
    gate = jnp.einsum('nthd,nbhd->nthb', qs.astype(jnp.float32), bmean)
    _, idx = lax.top_k(gate, k_eff)
    phys = blk_pages[jnp.arange(n)[:, None, None, None], idx]
    rows = jnp.arange(PAGE_SIZE)
    heads = jnp.arange(H)[None, None, :, None, None, None]
    k_sel = cache_k[layer, phys[..., None], rows, heads].reshape(n, L, H, k_eff * MOBA_BLOCK, dh)
    v_sel = cache_v[layer, phys[..., None], rows, heads].reshape(n, L, H, k_eff * MOBA_BLOCK, dh)
    l_sel = jnp.einsum('nthd,nthjd->nthj', qs, k_sel).astype(jnp.float32)
    p = jax.nn.softmax(jnp.concatenate([l_sel, l_own], axis=-1), axis=-1).astype(v_new.dtype)
    p_sel = p[..., :k_eff * MOBA_BLOCK]
    p_own = p[..., k_eff * MOBA_BLOCK:]
    out = (jnp.einsum('nthj,nthjd->nthd', p_sel, v_sel)
           + jnp.einsum('nthj,njhd->nthd', p_own, v_own))
    return out.reshape(n, L, H * dh)


def hier_moe(h, w_rg, b_rg, w_re, b_re, w_eg, w_eu, w_ed):
    n, L, D = h.shape
    t = h.reshape(n * L, D)
    glog = (jnp.einsum('td,dg->tg', t, w_rg) + b_rg).astype(jnp.float32)
    gprob = jax.nn.softmax(glog, axis=-1)
    gval, gidx = lax.top_k(glog, 1)
    gsel = gidx[:, 0]
    pg = jnp.take_along_axis(gprob, gidx, axis=1)[:, 0]
    elog_all = (jnp.einsum('td,gde->tge', t, w_re) + b_re).astype(jnp.float32)
    elog = jnp.take_along_axis(elog_all, gsel[:, None, None], axis=1)[:, 0]
    ev, ei = lax.top_k(elog, EXPERT_TOPK)
    ep = jax.nn.softmax(ev, axis=-1)
    w_e = jnp.sum(jax.nn.one_hot(ei, EXPERTS_PER_GROUP, dtype=jnp.float32) * ep[..., None], axis=1)
    w_full = jax.nn.one_hot(gsel, N_GROUPS, dtype=jnp.float32)[:, :, None] * (pg[:, None] * w_e)[:, None, :]
    w_full = w_full.astype(h.dtype)
    out = jnp.zeros((n * L, D), h.dtype)
    for g in range(N_GROUPS):
        a = jnp.einsum('td,edf->tef', t, w_eg[g])
        b = jnp.einsum('td,edf->tef', t, w_eu[g])
        hid = jax.nn.silu(a) * b * w_full[:, g, :, None]
        out = out + jnp.einsum('tef,efd->td', hid, w_ed[g])
    return out.reshape(n, L, D)


def sublayers(x, pool_y, attn_y, g_pool, g_attn, w_pool_proj, w_attn_proj, w_out, ln1_g, ln1_b,
              w_rg, b_rg, w_re, b_re, w_eg, w_eu, w_ed, ln2_g, ln2_b):
    merged = (g_pool * jnp.einsum('nlc,cd->nld', pool_y, w_pool_proj)
              + g_attn * jnp.einsum('nlc,cd->nld', attn_y, w_attn_proj))
    h = layer_norm(ALPHA * x + jnp.einsum('nld,de->nle', merged, w_out), ln1_g, ln1_b)
    return layer_norm(ALPHA * h + hier_moe(h, w_rg, b_rg, w_re, b_re, w_eg, w_eu, w_ed), ln2_g, ln2_b)


def setup_inputs(seed: int = 0) -> dict:
    key = jax.random.key(seed)
    ks = jax.random.split(key, 24)
    f32 = jnp.float32
    n_pages = PAST_LEN // PAGE_SIZE
    n_used = DEC_BATCH * n_pages
    n_pool = n_used + n_used // 4
    nrm = lambda k, s: jax.random.normal(k, s, f32)
    page_table = jax.random.permutation(ks[0], n_pool)[:n_used].reshape(DEC_BATCH, n_pages).astype(jnp.int32)
    E, G, F = EXPERTS_PER_GROUP, N_GROUPS, EXPERT_FF
    return {
        'x_prompt': nrm(ks[1], (BATCH, SEQ, D_MODEL)),
        'x_sample': nrm(ks[2], (DEC_BATCH, DEC_SEQ, D_MODEL)),
        'cache_k': nrm(ks[3], (DEPTH, n_pool, PAGE_SIZE, N_HEADS, HEAD_DIM)),
        'cache_v': nrm(ks[4], (DEPTH, n_pool, PAGE_SIZE, N_HEADS, HEAD_DIM)),
        'state_pool': nrm(ks[5], (DEPTH, DEC_BATCH, POOL_BUF, POOL_WIDTH)),
        'page_table': page_table,
        'w_in': nrm(ks[6], (DEPTH, D_MODEL, IN_WIDTH)) * D_MODEL ** -0.5,
        'w_pool_group': nrm(ks[7], (DEPTH, POOL_GROUPS, POOL_GROUP_WIDTH, POOL_GROUP_WIDTH)) * POOL_GROUP_WIDTH ** -0.5,
        'pool_scale': 1.0 + 0.1 * nrm(ks[8], (DEPTH, POOL_WIDTH)),
        'w_pool_proj': nrm(ks[9], (DEPTH, POOL_WIDTH, D_MODEL)) * POOL_WIDTH ** -0.5,
        'w_attn_proj': nrm(ks[10], (DEPTH, ATTN_WIDTH, D_MODEL)) * ATTN_WIDTH ** -0.5,
        'w_out': nrm(ks[11], (DEPTH, D_MODEL, D_MODEL)) * (D_MODEL ** -0.5 * BETA),
        'ln1_g': 1.0 + 0.02 * nrm(ks[12], (DEPTH, D_MODEL)),
        'ln1_b': 0.02 * nrm(ks[13], (DEPTH, D_MODEL)),
        'w_router_group': nrm(ks[14], (DEPTH, D_MODEL, G)) * D_MODEL ** -0.5,
        'b_router_group': 0.01 * nrm(ks[15], (DEPTH, G)),
        'w_router_expert': nrm(ks[16], (DEPTH, G, D_MODEL, E)) * D_MODEL ** -0.5,
        'b_router_expert': 0.01 * nrm(ks[17], (DEPTH, G, E)),
        'w_exp_gate': nrm(ks[18], (DEPTH, G, E, D_MODEL, F)) * D_MODEL ** -0.5,
        'w_exp_up': nrm(ks[19], (DEPTH, G, E, D_MODEL, F)) * D_MODEL ** -0.5,
        'w_exp_down': nrm(ks[20], (DEPTH, G, E, F, D_MODEL)) * (F ** -0.5 * BETA),
        'ln2_g': 1.0 + 0.02 * nrm(ks[21], (DEPTH, D_MODEL)),
        'ln2_b': 0.02 * nrm(ks[22], (DEPTH, D_MODEL)),
    }


def reference(x_prompt, x_sample, cache_k, cache_v, state_pool, page_table, w_in, w_pool_group,
              pool_scale, w_pool_proj, w_attn_proj, w_out, ln1_g, ln1_b, w_router_group,
              b_router_group, w_router_expert, b_router_expert, w_exp_gate, w_exp_up,
              w_exp_down, ln2_g, ln2_b):
    past_len = page_table.shape[1] * PAGE_SIZE
    page_means = jnp.mean(cache_k, axis=2, dtype=jnp.float32)
    xp, xs = x_prompt, x_sample
    kp_l, vp_l, pp_l, ks_l, vs_l, ps_l = [], [], [], [], [], []
    for layer in range(DEPTH):
        tail = (w_pool_proj[layer], w_attn_proj[layer], w_out[layer], ln1_g[layer], ln1_b[layer],
                w_router_group[layer], b_router_group[layer], w_router_expert[layer],
                b_router_expert[layer], w_exp_gate[layer], w_exp_up[layer], w_exp_down[layer],
                ln2_g[layer], ln2_b[layer])
        u, q, k, v, gp, ga = project(xp, w_in[layer], 0)
        prefix = jnp.zeros((xp.shape[0], POOL_BUF, POOL_WIDTH), xp.dtype)
        pool_y, pool_new = pool_mix(u, prefix, 0, w_pool_group[layer], pool_scale[layer])
        attn_y = moba_prompt(q, k, v)
        xp = sublayers(xp, pool_y, attn_y, gp, ga, *tail)
        kp_l.append(k)
        vp_l.append(v)
        pp_l.append(pool_new)
        us, qs, kn, vn, gps, gas = project(xs, w_in[layer], past_len)
        pool_ys, pool_news = pool_mix(us, state_pool[layer], past_len, w_pool_group[layer], pool_scale[layer])
        attn_ys = moba_sample(qs, kn, vn, cache_k, cache_v, page_means, page_table, layer)
        xs = sublayers(xs, pool_ys, attn_ys, gps, gas, *tail)
        ks_l.append(kn)
        vs_l.append(vn)
        ps_l.append(pool_news)
    return (xp, xs, jnp.stack(kp_l), jnp.stack(vp_l), jnp.stack(pp_l), jnp.stack(ks_l), jnp.stack(vs_l), jnp.stack(ps_l))
```

```python
import functools

import jax
import jax.numpy as jnp
from jax import lax
from jax.experimental import pallas as pl
from jax.experimental.pallas import tpu as pltpu

F32 = jnp.float32
BF16 = jnp.bfloat16

POOL_WINDOWS = (2, 4, 8, 16)
POOL_GROUP_WIDTH = 128
POOL_WIDTH = 512
POOL_BUF = 15
HEAD_DIM = 64
N_HEADS = 8
ATTN_WIDTH = 512
ROT_DIM = 16
ROPE_THETA = 500000.0
MOBA_BLOCK = 256
MOBA_TOPK = 3
PAGE_SIZE = 128
N_GROUPS = 4
EXPERTS_PER_GROUP = 8
EXPERT_TOPK = 2
DEPTH = 1
ALPHA = (2 * DEPTH) ** 0.25
LN_EPS = 1e-5
NEG_INF = -1e30

LANES = 128
SUBLANES = 8
HALO_ROWS = 16
VMEM_LIMIT = 56 * 1024 * 1024

ROUTER_LANES = 128
EXPERT_ROW0 = 8


def _cparams(sem):
    return pltpu.CompilerParams(dimension_semantics=sem, vmem_limit_bytes=VMEM_LIMIT)


def _dot(a, b):
    return jnp.dot(a, b, preferred_element_type=F32)


def _dot_nt(a, b):
    return lax.dot_general(a, b, (((1,), (1,)), ((), ())), preferred_element_type=F32)


def _split_bf16(x):
    hi = x.astype(BF16)
    lo = (x - hi.astype(F32)).astype(BF16)
    return hi, lo


def _dot3(a, b):
    ah, al = _split_bf16(a)
    bh, bl = _split_bf16(b)
    return _dot(ah, bh) + _dot(ah, bl) + _dot(al, bh)


def _sigmoid(x):
    return 1.0 / (1.0 + jnp.exp(-x))


def _layer_norm(x, g, b):
    mu = jnp.mean(x, axis=-1, keepdims=True)
    xc = x - mu
    var = jnp.mean(xc * xc, axis=-1, keepdims=True)
    return xc * lax.rsqrt(var + LN_EPS) * g + b


def _rotary_t(zt, cos, sin):
    half = ROT_DIM // 2
    parts = []
    for h in range(N_HEADS):
        r0 = h * HEAD_DIM
        x1 = zt[r0:r0 + half]
        x2 = zt[r0 + half:r0 + ROT_DIM]
        parts += [x1 * cos - x2 * sin, x2 * cos + x1 * sin, zt[r0 + ROT_DIM:r0 + HEAD_DIM]]
    return jnp.concatenate(parts, axis=0)


def _cos_sin(inv, pos_row):
    ang = inv * pos_row
    return jnp.cos(ang), jnp.sin(ang)


def _proj_prompt_kernel(x_ref, wn_ref, wt_ref, inv_ref,
                        pooled_ref, qt_ref, k_ref, kb_ref, kmean_ref, v_ref, vt_ref,
                        gp_ref, ga_ref, utail_ref, halo_ref, *, tm):
    i = pl.program_id(1)
    nblk = tm // MOBA_BLOCK

    @pl.when(i == 0)
    def _():
        halo_ref[...] = jnp.zeros_like(halo_ref)

    xb = x_ref[0].astype(BF16)
    zn = _dot(xb, wn_ref[...])
    zt = _dot_nt(wt_ref[...], xb)

    u = zn[:, :POOL_WIDTH]
    ext = jnp.concatenate([halo_ref[...], u], axis=0)
    pos1 = i * tm + lax.broadcasted_iota(jnp.int32, (tm, POOL_GROUP_WIDTH), 0) + 1
    outs = []
    for g, w in enumerate(POOL_WINDOWS):
        c0 = g * POOL_GROUP_WIDTH
        s = ext[:, c0:c0 + POOL_GROUP_WIDTH]
        k = 1
        while k < w:
            n = s.shape[0]
            s = s[k:] + s[:n - k]
            k *= 2
        s = s[HALO_ROWS + 1 - w:HALO_ROWS + 1 - w + tm]
        cnt = jnp.minimum(pos1, w).astype(F32)
        outs.append(s / cnt - u[:, c0:c0 + POOL_GROUP_WIDTH])
    pooled_ref[0] = jnp.concatenate(outs, axis=1).astype(BF16)
    halo_ref[...] = u[tm - HALO_ROWS:]
    utail_ref[0] = u[tm - HALO_ROWS:]

    v = zn[:, POOL_WIDTH:POOL_WIDTH + ATTN_WIDTH]
    v_ref[0] = v
    vt = v.T.astype(BF16)
    gates = _sigmoid(zn[:, POOL_WIDTH + ATTN_WIDTH:])
    d = gp_ref.shape[-1]
    gp_ref[0] = gates[:, :d]
    ga_ref[0] = gates[:, d:]

    pos = (i * tm + lax.broadcasted_iota(jnp.int32, (1, tm), 1)).astype(F32)
    cos, sin = _cos_sin(inv_ref[...], pos)
    qt = _rotary_t(zt[:ATTN_WIDTH], cos, sin) * (HEAD_DIM ** -0.5)
    kn = _rotary_t(zt[ATTN_WIDTH:], cos, sin).T
    k_ref[0] = kn
    kb_ref[0] = kn.astype(BF16)
    for j in range(nblk):
        sl = slice(j * MOBA_BLOCK, (j + 1) * MOBA_BLOCK)
        qt_ref[0, j] = qt[:, sl]
        vt_ref[0, j] = vt[:, sl]
        kmean_ref[0, pl.ds(i * nblk + j, 1), :] = (
            jnp.sum(kn[sl], axis=0, keepdims=True) * (1.0 / MOBA_BLOCK))


def _proj_prompt(x, wn, wt, inv, tm):
    n, s, d = x.shape
    nb = s // MOBA_BLOCK
    nblk = tm // MOBA_BLOCK
    tok = lambda c, dt: jax.ShapeDtypeStruct((n, s, c), dt)
    blk = lambda dt: jax.ShapeDtypeStruct((n, nb, ATTN_WIDTH, MOBA_BLOCK), dt)
    tok_spec = lambda c: pl.BlockSpec((1, tm, c), lambda b, i: (b, i, 0))
    blk_spec = pl.BlockSpec((1, nblk, ATTN_WIDTH, MOBA_BLOCK), lambda b, i: (b, i, 0, 0))
    full = lambda a: pl.BlockSpec(a.shape, lambda b, i: (0,) * a.ndim)
    return pl.pallas_call(
        functools.partial(_proj_prompt_kernel, tm=tm),
        grid=(n, s // tm),
        in_specs=[tok_spec(d), full(wn), full(wt), full(inv)],
        out_specs=[tok_spec(POOL_WIDTH), blk_spec, tok_spec(ATTN_WIDTH), tok_spec(ATTN_WIDTH),
                   pl.BlockSpec((1, nb, ATTN_WIDTH), lambda b, i: (b, 0, 0)),
                   tok_spec(ATTN_WIDTH), blk_spec, tok_spec(d), tok_spec(d),
                   pl.BlockSpec((1, HALO_ROWS, POOL_WIDTH), lambda b, i: (b, 0, 0))],
        out_shape=[tok(POOL_WIDTH, BF16), blk(F32), tok(ATTN_WIDTH, F32), tok(ATTN_WIDTH, BF16),
                   jax.ShapeDtypeStruct((n, nb, ATTN_WIDTH), F32),
                   tok(ATTN_WIDTH, F32), blk(BF16), tok(d, F32), tok(d, F32),
                   jax.ShapeDtypeStruct((n, HALO_ROWS, POOL_WIDTH), F32)],
        scratch_shapes=[pltpu.VMEM((HALO_ROWS, POOL_WIDTH), F32)],
        compiler_params=_cparams(("arbitrary", "arbitrary")),
        name="proj_prompt",
    )(x, wn, wt, inv)


def _proj_sample_kernel(x_ref, wn_ref, wt_ref, inv_ref, st_ref,
                        pooled_ref, u_ref, q_ref, k_ref, v_ref, gp_ref, ga_ref, *, past_len):
    xb = x_ref[...].astype(BF16)
    zn = _dot(xb, wn_ref[...])
    zt = _dot_nt(wt_ref[...], xb)
    m = xb.shape[0]

    u = zn[:, :POOL_WIDTH]
    u_ref[...] = u
    outs = []
    for g, w in enumerate(POOL_WINDOWS):
        c0 = g * POOL_GROUP_WIDTH
        s = u[:, c0:c0 + POOL_GROUP_WIDTH]
        for r in range(POOL_BUF - (w - 1), POOL_BUF):
            s = s + st_ref[r, :, c0:c0 + POOL_GROUP_WIDTH]
        cnt = float(min(past_len + 1, w))
        outs.append(s / cnt - u[:, c0:c0 + POOL_GROUP_WIDTH])
    pooled_ref[...] = jnp.concatenate(outs, axis=1).astype(BF16)

    v_ref[...] = zn[:, POOL_WIDTH:POOL_WIDTH + ATTN_WIDTH]
    gates = _sigmoid(zn[:, POOL_WIDTH + ATTN_WIDTH:])
    d = gp_ref.shape[-1]
    gp_ref[...] = gates[:, :d]
    ga_ref[...] = gates[:, d:]

    pos = jnp.full((1, m), float(past_len), F32)
    cos, sin = _cos_sin(inv_ref[...], pos)
    q_ref[...] = (_rotary_t(zt[:ATTN_WIDTH], cos, sin) * (HEAD_DIM ** -0.5)).T
    k_ref[...] = _rotary_t(zt[ATTN_WIDTH:], cos, sin).T


def _proj_sample(x, wn, wt, inv, state_t, past_len):
    m, d = x.shape
    o = lambda c, dt: jax.ShapeDtypeStruct((m, c), dt)
    return pl.pallas_call(
        functools.partial(_proj_sample_kernel, past_len=past_len),
        out_shape=[o(POOL_WIDTH, BF16), o(POOL_WIDTH, F32), o(ATTN_WIDTH, F32), o(ATTN_WIDTH, F32),
                   o(ATTN_WIDTH, F32), o(d, F32), o(d, F32)],
        compiler_params=pltpu.CompilerParams(vmem_limit_bytes=VMEM_LIMIT),
        name="proj_sample",
    )(x, wn, wt, inv, state_t)


def _moba_prompt_kernel(qt_ref, kb_ref, vt_ref, kmean_ref, out_ref, sel_ref):
    blk = pl.program_id(2)
    nb = kmean_ref.shape[1]
    two = 2 * MOBA_BLOCK

    qt = qt_ref[0, 0]
    rows = lax.broadcasted_iota(jnp.int32, qt.shape, 0)
    qbd = jnp.concatenate([jnp.where(rows < HEAD_DIM, qt, 0.0),
                           jnp.where(rows >= HEAD_DIM, qt, 0.0)], axis=1)
    qbd_b = qbd.astype(BF16)

    gate = _dot3(kmean_ref[0], qbd)
    jj = lax.broadcasted_iota(jnp.int32, gate.shape, 0)
    jf = jj.astype(F32)
    valid = jj < blk
    work = jnp.where(valid, gate, NEG_INF)
    sel = jnp.zeros(gate.shape, jnp.bool_)
    for _ in range(min(MOBA_TOPK, nb - 1)):
        mx = jnp.max(work, axis=0, keepdims=True)
        first = jnp.min(jnp.where(work == mx, jf, float(nb)), axis=0, keepdims=True)
        pick = jf == first
        sel = jnp.logical_or(sel, pick)
        work = jnp.where(pick, -jnp.inf, work)
    sel_ref[...] = jnp.where(jnp.logical_and(sel, valid), 1.0, 0.0)

    def per_head(row):
        a = jnp.broadcast_to(row[:, :MOBA_BLOCK], (HEAD_DIM, MOBA_BLOCK))
        b = jnp.broadcast_to(row[:, MOBA_BLOCK:], (HEAD_DIM, MOBA_BLOCK))
        return jnp.concatenate([a, b], axis=0)

    def pv(vt, p):
        pb = p.astype(BF16)
        return jnp.concatenate([_dot(vt[:HEAD_DIM], pb[:, :MOBA_BLOCK]),
                                _dot(vt[HEAD_DIM:], pb[:, MOBA_BLOCK:])], axis=0)

    def past_block(j, carry):
        m, l, acc = carry
        kj = kb_ref[0, pl.ds(pl.multiple_of(j * MOBA_BLOCK, MOBA_BLOCK), MOBA_BLOCK), :]
        st = _dot(kj, qbd_b)
        on = sel_ref[pl.ds(j, 1), :]
        mj = jnp.max(st, axis=0, keepdims=True)
        m_new = jnp.where(on > 0.0, jnp.maximum(m, mj), m)
        p = jnp.exp(st - jnp.where(on > 0.0, m_new, mj))
        alpha = jnp.exp(m - m_new)
        l = alpha * l + on * jnp.sum(p, axis=0, keepdims=True)
        acc = per_head(alpha) * acc + per_head(on) * pv(vt_ref[0, j], p)
        return m_new, l, acc

    init = (jnp.full((1, two), NEG_INF, F32), jnp.zeros((1, two), F32),
            jnp.zeros((2 * HEAD_DIM, MOBA_BLOCK), F32))
    m, l, acc = lax.fori_loop(0, blk, past_block, init)

    k_own = kb_ref[0, pl.ds(pl.multiple_of(blk * MOBA_BLOCK, MOBA_BLOCK), MOBA_BLOCK), :]
    st = _dot(k_own, qbd_b)
    kpos = lax.broadcasted_iota(jnp.int32, st.shape, 0)
    qpos = lax.broadcasted_iota(jnp.int32, st.shape, 1) % MOBA_BLOCK
    st = jnp.where(kpos <= qpos, st, NEG_INF)
    m_new = jnp.maximum(m, jnp.max(st, axis=0, keepdims=True))
    p = jnp.exp(st - m_new)
    alpha = jnp.exp(m - m_new)
    l = alpha * l + jnp.sum(p, axis=0, keepdims=True)
    acc = per_head(alpha) * acc + pv(vt_ref[0, blk], p)
    out_ref[0] = (acc / per_head(l)).T


def _moba_prompt(qt, kb, vt, kmean):
    n, nb, _, _ = qt.shape
    s = kb.shape[1]
    pair = 2 * HEAD_DIM
    npair = ATTN_WIDTH // pair
    return pl.pallas_call(
        _moba_prompt_kernel,
        grid=(n, npair, nb),
        in_specs=[pl.BlockSpec((1, 1, pair, MOBA_BLOCK), lambda b, p, j: (b, j, p, 0)),
                  pl.BlockSpec((1, s, pair), lambda b, p, j: (b, 0, p)),
                  pl.BlockSpec((1, nb, pair, MOBA_BLOCK), lambda b, p, j: (b, 0, p, 0)),
                  pl.BlockSpec((1, nb, pair), lambda b, p, j: (b, 0, p))],
        out_specs=pl.BlockSpec((1, MOBA_BLOCK, pair), lambda b, p, j: (b, j, p)),
        out_shape=jax.ShapeDtypeStruct((n, s, ATTN_WIDTH), F32),
        scratch_shapes=[pltpu.VMEM((nb, 2 * MOBA_BLOCK), F32)],
        compiler_params=_cparams(("arbitrary", "arbitrary", "arbitrary")),
        name="moba_prompt",
    )(qt, kb, vt, kmean)


def _sublayer1_kernel(x_ref, pooled_ref, attn_ref, gp_ref, ga_ref, wg_ref, scale_ref, wpp_ref, wap_ref,
                      wout_ref, g1_ref, b1_ref, wr_ref, br_ref, h_ref, w4_ref):
    tm = x_ref.shape[0]
    pooled = pooled_ref[...]
    py = jnp.concatenate(
        [_dot(pooled[:, g * POOL_GROUP_WIDTH:(g + 1) * POOL_GROUP_WIDTH], wg_ref[g])
         for g in range(len(POOL_WINDOWS))], axis=1) * scale_ref[...]
    a = _dot(py.astype(BF16), wpp_ref[...])
    b = _dot(attn_ref[...].astype(BF16), wap_ref[...])
    merged = gp_ref[...] * a + ga_ref[...] * b
    h = _layer_norm(ALPHA * x_ref[...] + _dot(merged.astype(BF16), wout_ref[...]),
                    g1_ref[...], b1_ref[...])
    h_ref[...] = h

    lt = (_dot3(h, wr_ref[...]) + br_ref[...]).T
    idx = lax.broadcasted_iota(jnp.int32, (SUBLANES, tm), 0)
    idf = idx.astype(F32)
    gl = jnp.where(idx < N_GROUPS, lt[:SUBLANES], -jnp.inf)
    gmax = jnp.max(gl, axis=0, keepdims=True)
    gexp = jnp.exp(gl - gmax)
    gprob = gexp / jnp.sum(gexp, axis=0, keepdims=True)
    gsel = jnp.min(jnp.where(gl == gmax, idf, float(SUBLANES)), axis=0, keepdims=True)
    pg = jnp.sum(jnp.where(idf == gsel, gprob, 0.0), axis=0, keepdims=True)
    elog = jnp.zeros((SUBLANES, tm), F32)
    for g in range(N_GROUPS):
        r0 = EXPERT_ROW0 + g * EXPERTS_PER_GROUP
        elog = jnp.where(gsel == float(g), lt[r0:r0 + EXPERTS_PER_GROUP], elog)
    e1 = jnp.max(elog, axis=0, keepdims=True)
    i1 = jnp.min(jnp.where(elog == e1, idf, float(SUBLANES)), axis=0, keepdims=True)
    rest = jnp.where(idf == i1, -jnp.inf, elog)
    e2 = jnp.max(rest, axis=0, keepdims=True)
    i2 = jnp.min(jnp.where(rest == e2, idf, float(SUBLANES)), axis=0, keepdims=True)
    x2 = jnp.exp(e2 - e1)
    den = 1.0 + x2
    w_e = jnp.where(idf == i1, 1.0 / den, 0.0) + jnp.where(idf == i2, x2 / den, 0.0)
    pw = pg * w_e
    slabs = [jnp.where(gsel == float(g), pw, 0.0) for g in range(N_GROUPS)]
    slabs.append(jnp.zeros((ROUTER_LANES - N_GROUPS * EXPERTS_PER_GROUP, tm), F32))
    wn = jnp.concatenate(slabs, axis=0).T
    for g in range(N_GROUPS):
        sh = (ROUTER_LANES - g * EXPERTS_PER_GROUP) % ROUTER_LANES
        w4_ref[g] = wn if sh == 0 else pltpu.roll(wn, sh, 1)


def _sublayer1(x, pooled, attn, gp, ga, wts, tm):
    t, d = x.shape
    wg, scale, wpp, wap, wout, g1, b1, wr, br = wts
    row = lambda c: pl.BlockSpec((tm, c), lambda i: (i, 0))
    full = lambda a: pl.BlockSpec(a.shape, lambda i: (0,) * a.ndim)
    return pl.pallas_call(
        _sublayer1_kernel,
        grid=(t // tm,),
        in_specs=[row(d), row(POOL_WIDTH), row(ATTN_WIDTH), row(d), row(d)] + [full(a) for a in wts],
        out_specs=[row(d), pl.BlockSpec((N_GROUPS, tm, ROUTER_LANES), lambda i: (0, i, 0))],
        out_shape=[jax.ShapeDtypeStruct((t, d), F32),
                   jax.ShapeDtypeStruct((N_GROUPS, t, ROUTER_LANES), F32)],
        compiler_params=_cparams(("arbitrary",)),
        name="sublayer1",
    )(x, pooled, attn, gp, ga, *wts)


def _moe_kernel(h_ref, w_ref, weg_ref, weu_ref, wed_ref, g2_ref, b2_ref, y_ref, acc_ref):
    g = pl.program_id(1)

    @pl.when(g == 0)
    def _():
        acc_ref[...] = jnp.zeros_like(acc_ref)

    h = h_ref[...]
    hb = h.astype(BF16)
    w = w_ref[0]
    hid = []
    for e in range(EXPERTS_PER_GROUP):
        a = _dot(hb, weg_ref[0, e])
        b = _dot(hb, weu_ref[0, e])
        hid.append((a * _sigmoid(a) * b * w[:, e:e + 1]).astype(BF16))
    acc_ref[...] += _dot(jnp.concatenate(hid, axis=1), wed_ref[0])

    @pl.when(g == pl.num_programs(1) - 1)
    def _():
        y_ref[...] = _layer_norm(ALPHA * h + acc_ref[...], g2_ref[...], b2_ref[...])


def _moe(h, w4, weg, weu, wed, g2, b2, tm):
    t, d = h.shape
    ng, ne, _, f = weg.shape
    return pl.pallas_call(
        _moe_kernel,
        grid=(t // tm, ng),
        in_specs=[pl.BlockSpec((tm, d), lambda i, g: (i, 0)),
                  pl.BlockSpec((1, tm, ROUTER_LANES), lambda i, g: (g, i, 0)),
                  pl.BlockSpec((1, ne, d, f), lambda i, g: (g, 0, 0, 0)),
                  pl.BlockSpec((1, ne, d, f), lambda i, g: (g, 0, 0, 0)),
                  pl.BlockSpec((1, ne * f, d), lambda i, g: (g, 0, 0)),
                  pl.BlockSpec(g2.shape, lambda i, g: (0, 0)),
                  pl.BlockSpec(b2.shape, lambda i, g: (0, 0))],
        out_specs=pl.BlockSpec((tm, d), lambda i, g: (i, 0)),
        out_shape=jax.ShapeDtypeStruct((t, d), F32),
        scratch_shapes=[pltpu.VMEM((tm, d), F32)],
        compiler_params=_cparams(("arbitrary", "arbitrary")),
        name="moe",
    )(h, w4, weg, weu, wed, g2, b2)


PAGE_RING = 8


def _block_mean_kernel(pt_ref, ck_ref, out_ref, buf_ref, sem_ref, *, n_pages):
    n = pl.program_id(0)
    nseq = pl.num_programs(0)
    ppb = MOBA_BLOCK // PAGE_SIZE

    def page_copy(page, slot):
        return pltpu.make_async_copy(ck_ref.at[page], buf_ref.at[slot], sem_ref.at[slot])

    @pl.when(n == 0)
    def _():
        for s in range(PAGE_RING):
            page_copy(pt_ref[0, s], s).start()

    first = None
    for i in range(n_pages):
        slot = i % PAGE_RING
        page_copy(pt_ref[n, i], slot).wait()
        psum = jnp.sum(buf_ref[slot], axis=0, keepdims=True)
        nxt = i + PAGE_RING
        if nxt < n_pages:
            page_copy(pt_ref[n, nxt], slot).start()
        else:
            @pl.when(n + 1 < nseq)
            def _():
                page_copy(pt_ref[n + 1, nxt - n_pages], slot).start()
        if i % ppb == 0:
            first = psum
        else:
            first = first + psum
        if i % ppb == ppb - 1:
            out_ref[0, pl.ds(i // ppb, 1), :] = first * (1.0 / MOBA_BLOCK)


def _block_means(page_table, cache_k3):
    nseq, n_pages = page_table.shape
    _, rows, width = cache_k3.shape
    nblk = n_pages * PAGE_SIZE // MOBA_BLOCK
    return pl.pallas_call(
        functools.partial(_block_mean_kernel, n_pages=n_pages),
        grid_spec=pltpu.PrefetchScalarGridSpec(
            num_scalar_prefetch=1,
            grid=(nseq,),
            in_specs=[pl.BlockSpec(memory_space=pl.ANY)],
            out_specs=pl.BlockSpec((1, nblk, width), lambda n, pt: (n, 0, 0)),
            scratch_shapes=[pltpu.VMEM((PAGE_RING, rows, width), F32),
                            pltpu.SemaphoreType.DMA((PAGE_RING,))]),
        out_shape=jax.ShapeDtypeStruct((nseq, nblk, width), F32),
        compiler_params=_cparams(("arbitrary",)),
        name="block_means",
    )(page_table, cache_k3)


SEQ_TILE = 8


def _block_topk_kernel(q_ref, bm_ref, seg_ref, idx_ref):
    nblk = bm_ref.shape[1]
    jf = lax.broadcasted_iota(jnp.int32, (nblk, LANES), 0).astype(F32)
    for s in range(SEQ_TILE):
        gate = _dot3(bm_ref[s] * q_ref[s], seg_ref[...])
        work = gate
        rows = []
        for _ in range(min(MOBA_TOPK, nblk)):
            mx = jnp.max(work, axis=0, keepdims=True)
            first = jnp.min(jnp.where(work == mx, jf, float(nblk)), axis=0, keepdims=True)
            rows.append(first)
            work = jnp.where(jf == first, -jnp.inf, work)
        rows.append(jnp.zeros((SUBLANES - len(rows), LANES), F32))
        idx_ref[s] = jnp.concatenate(rows, axis=0).astype(jnp.int32)


def _block_topk(q3, bmean, seg):
    nseq, nblk, width = bmean.shape
    return pl.pallas_call(
        _block_topk_kernel,
        grid=(nseq // SEQ_TILE,),
        in_specs=[pl.BlockSpec((SEQ_TILE, 1, width), lambda i: (i, 0, 0)),
                  pl.BlockSpec((SEQ_TILE, nblk, width), lambda i: (i, 0, 0)),
                  pl.BlockSpec(seg.shape, lambda i: (0, 0))],
        out_specs=pl.BlockSpec((SEQ_TILE, SUBLANES, LANES), lambda i: (i, 0, 0)),
        out_shape=jax.ShapeDtypeStruct((nseq, SUBLANES, LANES), jnp.int32),
        compiler_params=_cparams(("arbitrary",)),
        name="block_topk",
    )(q3, bmean, seg)


def _moba_sample_kernel(pt_ref, idx_ref, q_ref, kn_ref, vn_ref, ck_ref, cv_ref, out_ref,
                        kbuf_ref, vbuf_ref, sem_ref, *, k_eff):
    n = pl.program_id(0)
    nseq = pl.num_programs(0)
    ppb = MOBA_BLOCK // PAGE_SIZE
    n_sel = k_eff * MOBA_BLOCK
    rows = kbuf_ref.shape[2]

    def copies(seq, slot):
        out = []
        for h in range(N_HEADS):
            for k in range(k_eff):
                blk = idx_ref[seq, k, h]
                for j in range(ppb):
                    page = pt_ref[seq, blk * ppb + j]
                    dst = pl.ds((k * ppb + j) * PAGE_SIZE, PAGE_SIZE)
                    out.append(pltpu.make_async_copy(
                        ck_ref.at[page, :, h, :], kbuf_ref.at[slot, h, dst, :], sem_ref.at[slot, 0]))
                    out.append(pltpu.make_async_copy(
                        cv_ref.at[page, :, h, :], vbuf_ref.at[slot, h, dst, :], sem_ref.at[slot, 1]))
        return out

    slot = n % 2

    @pl.when(n == 0)
    def _():
        for c in copies(0, 0):
            c.start()

    @pl.when(n + 1 < nseq)
    def _():
        for c in copies(n + 1, 1 - slot):
            c.start()

    for c in copies(n, slot):
        c.wait()

    q = q_ref[0]
    kn = kn_ref[0]
    vn = vn_ref[0]
    ridx = lax.broadcasted_iota(jnp.int32, (rows, LANES), 0)
    outs = []
    for h in range(N_HEADS):
        c0 = h * HEAD_DIM
        kbuf_ref[slot, h, n_sel:, :] = jnp.broadcast_to(kn[:, c0:c0 + HEAD_DIM], (rows - n_sel, HEAD_DIM))
        vbuf_ref[slot, h, n_sel:, :] = jnp.broadcast_to(vn[:, c0:c0 + HEAD_DIM], (rows - n_sel, HEAD_DIM))
        qm = jnp.broadcast_to(q[:, c0:c0 + HEAD_DIM], (LANES, HEAD_DIM)).astype(BF16)
        st = _dot_nt(kbuf_ref[slot, h].astype(BF16), qm)
        st = jnp.where(ridx <= n_sel, st, NEG_INF)
        m = jnp.max(st, axis=0, keepdims=True)
        p = jnp.exp(st - m)
        l = jnp.sum(p, axis=0, keepdims=True)
        pv = jnp.sum(p[:, :HEAD_DIM] * vbuf_ref[slot, h], axis=0, keepdims=True)
        outs.append(pv / l[:, :HEAD_DIM])
    out_ref[0] = jnp.concatenate(outs, axis=1)


def _moba_sample(page_table, idx, q3, kn3, vn3, cache_k4, cache_v4, k_eff):
    nseq = q3.shape[0]
    width = q3.shape[2]
    rows = k_eff * MOBA_BLOCK + SUBLANES
    row_spec = pl.BlockSpec((1, 1, width), lambda n, pt, ix: (n, 0, 0))
    return pl.pallas_call(
        functools.partial(_moba_sample_kernel, k_eff=k_eff),
        grid_spec=pltpu.PrefetchScalarGridSpec(
            num_scalar_prefetch=2,
            grid=(nseq,),
            in_specs=[row_spec, row_spec, row_spec,
                      pl.BlockSpec(memory_space=pl.ANY), pl.BlockSpec(memory_space=pl.ANY)],
            out_specs=row_spec,
            scratch_shapes=[pltpu.VMEM((2, N_HEADS, rows, HEAD_DIM), F32),
                            pltpu.VMEM((2, N_HEADS, rows, HEAD_DIM), F32),
                            pltpu.SemaphoreType.DMA((2, 2))]),
        out_shape=jax.ShapeDtypeStruct((nseq, 1, width), F32),
        compiler_params=_cparams(("arbitrary",)),
        name="moba_sample",
    )(page_table, idx, q3, kn3, vn3, cache_k4, cache_v4)


def kernel(x_prompt, x_sample, cache_k, cache_v, state_pool, page_table, w_in, w_pool_group,
           pool_scale, w_pool_proj, w_attn_proj, w_out, ln1_g, ln1_b, w_router_group,
           b_router_group, w_router_expert, b_router_expert, w_exp_gate, w_exp_up,
           w_exp_down, ln2_g, ln2_b):
    assert w_in.shape[0] == DEPTH == 1
    n, s, d = x_prompt.shape
    nd, ld, _ = x_sample.shape
    assert ld == 1
    past_len = page_table.shape[1] * PAGE_SIZE
    assert past_len % MOBA_BLOCK == 0 and past_len // MOBA_BLOCK >= MOBA_TOPK
    layer = 0

    a0 = POOL_WIDTH
    win = w_in[layer]
    wn = jnp.concatenate([win[:, :a0], win[:, a0 + 2 * ATTN_WIDTH:]], axis=1).astype(BF16)
    wt = win[:, a0:a0 + 2 * ATTN_WIDTH].T.astype(BF16)
    half = ROT_DIM // 2
    inv = (ROPE_THETA ** (-2.0 * jnp.arange(half, dtype=F32) / ROT_DIM)).reshape(half, 1)
    row = lambda a: a.reshape(1, -1)
    ng, ne = N_GROUPS, EXPERTS_PER_GROUP
    wr = jnp.zeros((d, ROUTER_LANES), F32)
    wr = wr.at[:, :ng].set(w_router_group[layer])
    wr = wr.at[:, EXPERT_ROW0:EXPERT_ROW0 + ng * ne].set(
        jnp.transpose(w_router_expert[layer], (1, 0, 2)).reshape(d, ng * ne))
    br = jnp.zeros((1, ROUTER_LANES), F32)
    br = br.at[0, :ng].set(b_router_group[layer])
    br = br.at[0, EXPERT_ROW0:EXPERT_ROW0 + ng * ne].set(b_router_expert[layer].reshape(-1))
    sub_w = (w_pool_group[layer].astype(BF16), row(pool_scale[layer]), w_pool_proj[layer].astype(BF16),
             w_attn_proj[layer].astype(BF16), w_out[layer].astype(BF16), row(ln1_g[layer]),
             row(ln1_b[layer]), wr, br)
    weg = w_exp_gate[layer].astype(BF16)
    weu = w_exp_up[layer].astype(BF16)
    f = weg.shape[-1]
    wed = w_exp_down[layer].astype(BF16).reshape(ng, ne * f, d)
    g2, b2 = row(ln2_g[layer]), row(ln2_b[layer])

    tm = 512
    pooled, qt, k_p, kb, kmean, v_p, vt, gp, ga, utail = _proj_prompt(x_prompt, wn, wt, inv, tm)
    attn = _moba_prompt(qt, kb, vt, kmean)
    t = n * s
    flat = lambda a: a.reshape(t, a.shape[-1])
    h_p, w4_p = _sublayer1(flat(x_prompt), flat(pooled), flat(attn), flat(gp), flat(ga), sub_w, tm)
    y_p = _moe(h_p, w4_p, weg, weu, wed, g2, b2, tm).reshape(n, s, d)

    xs = x_sample.reshape(nd, d)
    state_t = jnp.transpose(state_pool[layer], (1, 0, 2))
    pooled_s, u_s, q_s, k_s, v_s, gp_s, ga_s = _proj_sample(xs, wn, wt, inv, state_t, past_len)
    n_pool = cache_k.shape[1]
    bmean = _block_means(page_table, cache_k[layer].reshape(n_pool, PAGE_SIZE, ATTN_WIDTH))
    seg = (jnp.arange(ATTN_WIDTH)[:, None] // HEAD_DIM == jnp.arange(LANES)[None, :]).astype(F32)
    r3 = lambda a: a.reshape(nd, 1, a.shape[-1])
    k_eff = MOBA_TOPK
    idx = _block_topk(r3(q_s), bmean, seg)[:, :k_eff, :N_HEADS]
    attn_s = _moba_sample(page_table, idx, r3(q_s), r3(k_s), r3(v_s), cache_k[layer], cache_v[layer],
                          k_eff).reshape(nd, ATTN_WIDTH)
    tms = nd
    h_s, w4_s = _sublayer1(xs, pooled_s, attn_s, gp_s, ga_s, sub_w, tms)
    y_s = _moe(h_s, w4_s, weg, weu, wed, g2, b2, tms).reshape(nd, 1, d)

    heads = lambda a, lead: a.reshape(lead + (N_HEADS, HEAD_DIM))
    pool_prompt = utail[:, HALO_ROWS - POOL_BUF:][None]
    pool_sample = jnp.concatenate([state_pool[layer][:, 1:], u_s[:, None, :]], axis=1)[None]
    return (y_p, y_s.reshape(nd, ld, d),
            heads(k_p, (1, n, s)), heads(v_p, (1, n, s)), pool_prompt,
            heads(k_s, (1, nd, 1)), heads(v_s, (1, nd, 1)), pool_sample)
```

```python
import functools

import jax
import jax.numpy as jnp
from jax import lax
from jax.experimental import pallas as pl
from jax.experimental.pallas import tpu as pltpu

F32 = jnp.float32
BF16 = jnp.bfloat16

POOL_WINDOWS = (2, 4, 8, 16)
POOL_GROUP_WIDTH = 128
POOL_WIDTH = 512
POOL_BUF = 15
HEAD_DIM = 64
N_HEADS = 8
ATTN_WIDTH = 512
ROT_DIM = 16
ROPE_THETA = 500000.0
MOBA_BLOCK = 256
MOBA_TOPK = 3
PAGE_SIZE = 128
N_GROUPS = 4
EXPERTS_PER_GROUP = 8
EXPERT_TOPK = 2
DEPTH = 1
ALPHA = (2 * DEPTH) ** 0.25
LN_EPS = 1e-5
NEG_INF = -1e30

LANES = 128
SUBLANES = 8
HALO_ROWS = 16
VMEM_LIMIT = 56 * 1024 * 1024

ROUTER_LANES = 128
EXPERT_ROW0 = 8
PAIR = 2 * HEAD_DIM
PAGE_RING = 8


def _cparams(sem):
    return pltpu.CompilerParams(dimension_semantics=sem, vmem_limit_bytes=VMEM_LIMIT)


def _dot(a, b):
    return jnp.dot(a, b, preferred_element_type=F32)


def _dot_nt(a, b):
    return lax.dot_general(a, b, (((1,), (1,)), ((), ())), preferred_element_type=F32)


def _split_bf16(x):
    hi = x.astype(BF16)
    lo = (x - hi.astype(F32)).astype(BF16)
    return hi, lo


def _dot3(a, b, dot=_dot):
    ah, al = _split_bf16(a)
    bh, bl = _split_bf16(b)
    return dot(ah, bh) + dot(ah, bl) + dot(al, bh)


def _dot_bf16(a, b):
    return _dot(a.astype(BF16), b.astype(BF16))


def _lane_sums_as_rows(x):
    ones = jnp.ones((SUBLANES, LANES), BF16)
    hi, lo = _split_bf16(x)
    return _dot_nt(ones, hi) + _dot_nt(ones, lo)


def _sigmoid(x):
    return 1.0 / (1.0 + jnp.exp(-x))


def _layer_norm(x, g, b):
    mu = jnp.mean(x, axis=-1, keepdims=True)
    xc = x - mu
    var = jnp.mean(xc * xc, axis=-1, keepdims=True)
    return xc * lax.rsqrt(var + LN_EPS) * g + b


def _rotary_t(zt, cos, sin):
    half = ROT_DIM // 2
    parts = []
    for h in range(N_HEADS):
        r0 = h * HEAD_DIM
        x1 = zt[r0:r0 + half]
        x2 = zt[r0 + half:r0 + ROT_DIM]
        parts += [x1 * cos - x2 * sin, x2 * cos + x1 * sin, zt[r0 + ROT_DIM:r0 + HEAD_DIM]]
    return jnp.concatenate(parts, axis=0)


def _cos_sin(inv, pos_row):
    ang = inv * pos_row
    return jnp.cos(ang), jnp.sin(ang)


def _first_argmax(work, idxf, axis, n):
    mx = jnp.max(work, axis=axis, keepdims=True)
    first = jnp.min(jnp.where(work == mx, idxf, float(n)), axis=axis, keepdims=True)
    return mx, first


def _proj_prompt_kernel(x_ref, wn_ref, wt_ref, inv_ref,
                        pooled_ref, qt_ref, k_ref, kb_ref, kmean_ref, v_ref, vt_ref,
                        gp_ref, ga_ref, utail_ref, halo_ref, *, tm):
    i = pl.program_id(1)
    nblk = tm // MOBA_BLOCK

    @pl.when(i == 0)
    def _():
        halo_ref[...] = jnp.zeros_like(halo_ref)

    xb = x_ref[0].astype(BF16)
    zn = _dot(xb, wn_ref[...])
    zt = _dot_nt(wt_ref[...], xb)

    u = zn[:, :POOL_WIDTH]
    ext = jnp.concatenate([halo_ref[...], u], axis=0)
    pos1 = i * tm + lax.broadcasted_iota(jnp.int32, (tm, POOL_GROUP_WIDTH), 0) + 1
    outs = []
    for g, w in enumerate(POOL_WINDOWS):
        c0 = g * POOL_GROUP_WIDTH
        s = ext[:, c0:c0 + POOL_GROUP_WIDTH]
        k = 1
        while k < w:
            n = s.shape[0]
            s = s[k:] + s[:n - k]
            k *= 2
        s = s[HALO_ROWS + 1 - w:HALO_ROWS + 1 - w + tm]
        cnt = jnp.minimum(pos1, w).astype(F32)
        outs.append(s / cnt - u[:, c0:c0 + POOL_GROUP_WIDTH])
    pooled_ref[0] = jnp.concatenate(outs, axis=1).astype(BF16)
    halo_ref[...] = u[tm - HALO_ROWS:]
    utail_ref[0] = u[tm - HALO_ROWS:]

    v = zn[:, POOL_WIDTH:POOL_WIDTH + ATTN_WIDTH]
    v_ref[0] = v
    vt = v.T.astype(BF16)
    gates = _sigmoid(zn[:, POOL_WIDTH + ATTN_WIDTH:])
    d = gp_ref.shape[-1]
    gp_ref[0] = gates[:, :d]
    ga_ref[0] = gates[:, d:]

    pos = (i * tm + lax.broadcasted_iota(jnp.int32, (1, tm), 1)).astype(F32)
    cos, sin = _cos_sin(inv_ref[...], pos)
    qt = _rotary_t(zt[:ATTN_WIDTH], cos, sin) * (HEAD_DIM ** -0.5)
    kn = _rotary_t(zt[ATTN_WIDTH:], cos, sin).T
    k_ref[0] = kn
    kb_ref[0] = kn.astype(BF16)
    for j in range(nblk):
        sl = slice(j * MOBA_BLOCK, (j + 1) * MOBA_BLOCK)
        qt_ref[0, j] = qt[:, sl]
        vt_ref[0, j] = vt[:, sl]
        kmean_ref[0, pl.ds(i * nblk + j, 1), :] = (
            jnp.sum(kn[sl], axis=0, keepdims=True) * (1.0 / MOBA_BLOCK))


def _proj_prompt(x, wn, wt, inv, tm):
    n, s, d = x.shape
    nb = s // MOBA_BLOCK
    nblk = tm // MOBA_BLOCK
    tok = lambda c, dt: jax.ShapeDtypeStruct((n, s, c), dt)
    blk = lambda dt: jax.ShapeDtypeStruct((n, nb, ATTN_WIDTH, MOBA_BLOCK), dt)
    tok_spec = lambda c: pl.BlockSpec((1, tm, c), lambda b, i: (b, i, 0))
    blk_spec = pl.BlockSpec((1, nblk, ATTN_WIDTH, MOBA_BLOCK), lambda b, i: (b, i, 0, 0))
    full = lambda a: pl.BlockSpec(a.shape, lambda b, i: (0,) * a.ndim)
    return pl.pallas_call(
        functools.partial(_proj_prompt_kernel, tm=tm),
        grid=(n, s // tm),
        in_specs=[tok_spec(d), full(wn), full(wt), full(inv)],
        out_specs=[tok_spec(POOL_WIDTH), blk_spec, tok_spec(ATTN_WIDTH), tok_spec(ATTN_WIDTH),
                   pl.BlockSpec((1, nb, ATTN_WIDTH), lambda b, i: (b, 0, 0)),
                   tok_spec(ATTN_WIDTH), blk_spec, tok_spec(d), tok_spec(d),
                   pl.BlockSpec((1, HALO_ROWS, POOL_WIDTH), lambda b, i: (b, 0, 0))],
        out_shape=[tok(POOL_WIDTH, BF16), blk(F32), tok(ATTN_WIDTH, F32), tok(ATTN_WIDTH, BF16),
                   jax.ShapeDtypeStruct((n, nb, ATTN_WIDTH), F32),
                   tok(ATTN_WIDTH, F32), blk(BF16), tok(d, F32), tok(d, F32),
                   jax.ShapeDtypeStruct((n, HALO_ROWS, POOL_WIDTH), F32)],
        scratch_shapes=[pltpu.VMEM((HALO_ROWS, POOL_WIDTH), F32)],
        compiler_params=_cparams(("arbitrary", "arbitrary")),
        name="proj_prompt",
    )(x, wn, wt, inv)


def _proj_sample_kernel(x_ref, wn_ref, wt_ref, inv_ref, st_ref,
                        pooled_ref, u_ref, q_ref, k_ref, v_ref, gp_ref, ga_ref, *, past_len):
    x = x_ref[...]
    uv_w = POOL_WIDTH + ATTN_WIDTH
    zn = _dot3(x, wn_ref[:, :uv_w])
    zt = _dot3(wt_ref[...], x, _dot_nt)
    m = x.shape[0]

    u = zn[:, :POOL_WIDTH]
    u_ref[...] = u
    outs = []
    for g, w in enumerate(POOL_WINDOWS):
        c0 = g * POOL_GROUP_WIDTH
        s = u[:, c0:c0 + POOL_GROUP_WIDTH]
        for r in range(POOL_BUF - (w - 1), POOL_BUF):
            s = s + st_ref[r, :, c0:c0 + POOL_GROUP_WIDTH]
        cnt = float(min(past_len + 1, w))
        outs.append(s / cnt - u[:, c0:c0 + POOL_GROUP_WIDTH])
    pooled_ref[...] = jnp.concatenate(outs, axis=1)

    v_ref[...] = zn[:, POOL_WIDTH:]
    gates = _sigmoid(_dot_bf16(x, wn_ref[:, uv_w:]))
    d = gp_ref.shape[-1]
    gp_ref[...] = gates[:, :d]
    ga_ref[...] = gates[:, d:]

    pos = jnp.full((1, m), float(past_len), F32)
    cos, sin = _cos_sin(inv_ref[...], pos)
    q_ref[...] = (_rotary_t(zt[:ATTN_WIDTH], cos, sin) * (HEAD_DIM ** -0.5)).T
    k_ref[...] = _rotary_t(zt[ATTN_WIDTH:], cos, sin).T


def _proj_sample(x, wn, wt, inv, state_t, past_len):
    m, d = x.shape
    o = lambda c: jax.ShapeDtypeStruct((m, c), F32)
    return pl.pallas_call(
        functools.partial(_proj_sample_kernel, past_len=past_len),
        out_shape=[o(POOL_WIDTH), o(POOL_WIDTH), o(ATTN_WIDTH), o(ATTN_WIDTH),
                   o(ATTN_WIDTH), o(d), o(d)],
        compiler_params=pltpu.CompilerParams(vmem_limit_bytes=VMEM_LIMIT),
        name="proj_sample",
    )(x, wn, wt, inv, state_t)


def _moba_prompt_kernel(qt_ref, kb_ref, vt_ref, kmean_ref, out_ref,
                        sel_ref, qbd_ref, m_ref, l_ref, acc_ref):
    blk = pl.program_id(1)
    nb = kmean_ref.shape[1]
    npair = ATTN_WIDTH // PAIR
    two = 2 * MOBA_BLOCK
    rows = lax.broadcasted_iota(jnp.int32, (PAIR, MOBA_BLOCK), 0)
    jj = lax.broadcasted_iota(jnp.int32, (nb, two), 0)
    jf = jj.astype(F32)
    valid = jj < blk

    for p in range(npair):
        ps = slice(p * PAIR, (p + 1) * PAIR)
        qt = qt_ref[0, 0, ps, :]
        qbd = jnp.concatenate([jnp.where(rows < HEAD_DIM, qt, 0.0),
                               jnp.where(rows >= HEAD_DIM, qt, 0.0)], axis=1)
        qbd_ref[p] = qbd.astype(BF16)
        work = jnp.where(valid, _dot3(kmean_ref[0, :, ps], qbd), NEG_INF)
        sel = jnp.zeros((nb, two), jnp.bool_)
        for _ in range(min(MOBA_TOPK, nb - 1)):
            _, first = _first_argmax(work, jf, 0, nb)
            pick = jf == first
            sel = jnp.logical_or(sel, pick)
            work = jnp.where(pick, -jnp.inf, work)
        sel_ref[p] = jnp.where(jnp.logical_and(sel, valid), 1.0, 0.0)
        m_ref[p] = jnp.full((1, two), NEG_INF, F32)
        l_ref[p] = jnp.zeros((1, two), F32)
        acc_ref[p] = jnp.zeros((PAIR, MOBA_BLOCK), F32)

    def per_head(row):
        a = jnp.broadcast_to(row[:, :MOBA_BLOCK], (HEAD_DIM, MOBA_BLOCK))
        b = jnp.broadcast_to(row[:, MOBA_BLOCK:], (HEAD_DIM, MOBA_BLOCK))
        return jnp.concatenate([a, b], axis=0)

    def pv(vt, p):
        pb = p.astype(BF16)
        return jnp.concatenate([_dot(vt[:HEAD_DIM], pb[:, :MOBA_BLOCK]),
                                _dot(vt[HEAD_DIM:], pb[:, MOBA_BLOCK:])], axis=0)

    def past_block(j, carry):
        kj = kb_ref[0, pl.ds(pl.multiple_of(j * MOBA_BLOCK, MOBA_BLOCK), MOBA_BLOCK), :]
        for p in range(npair):
            ps = slice(p * PAIR, (p + 1) * PAIR)
            st = _dot(kj[:, ps], qbd_ref[p])
            on = sel_ref[p, pl.ds(j, 1), :]
            m = m_ref[p]
            mj = jnp.max(st, axis=0, keepdims=True)
            m_new = jnp.where(on > 0.0, jnp.maximum(m, mj), m)
            pr = jnp.exp(st - jnp.where(on > 0.0, m_new, mj))
            alpha = jnp.exp(m - m_new)
            m_ref[p] = m_new
            l_ref[p] = alpha * l_ref[p] + on * jnp.sum(pr, axis=0, keepdims=True)
            acc_ref[p] = per_head(alpha) * acc_ref[p] + per_head(on) * pv(vt_ref[0, j, ps, :], pr)
        return carry

    lax.fori_loop(0, blk, past_block, 0)

    k_own = kb_ref[0, pl.ds(pl.multiple_of(blk * MOBA_BLOCK, MOBA_BLOCK), MOBA_BLOCK), :]
    kpos = lax.broadcasted_iota(jnp.int32, (MOBA_BLOCK, two), 0)
    qpos = lax.broadcasted_iota(jnp.int32, (MOBA_BLOCK, two), 1) % MOBA_BLOCK
    causal = kpos <= qpos
    for p in range(npair):
        ps = slice(p * PAIR, (p + 1) * PAIR)
        st = jnp.where(causal, _dot(k_own[:, ps], qbd_ref[p]), NEG_INF)
        m = m_ref[p]
        m_new = jnp.maximum(m, jnp.max(st, axis=0, keepdims=True))
        pr = jnp.exp(st - m_new)
        alpha = jnp.exp(m - m_new)
        l = alpha * l_ref[p] + jnp.sum(pr, axis=0, keepdims=True)
        acc = per_head(alpha) * acc_ref[p] + pv(vt_ref[0, blk, ps, :], pr)
        out_ref[0, :, ps] = (acc / per_head(l)).T


def _moba_prompt(qt, kb, vt, kmean):
    n, nb, _, _ = qt.shape
    s = kb.shape[1]
    npair = ATTN_WIDTH // PAIR
    return pl.pallas_call(
        _moba_prompt_kernel,
        grid=(n, nb),
        in_specs=[pl.BlockSpec((1, 1, ATTN_WIDTH, MOBA_BLOCK), lambda b, j: (b, j, 0, 0)),
                  pl.BlockSpec((1, s, ATTN_WIDTH), lambda b, j: (b, 0, 0)),
                  pl.BlockSpec((1, nb, ATTN_WIDTH, MOBA_BLOCK), lambda b, j: (b, 0, 0, 0)),
                  pl.BlockSpec((1, nb, ATTN_WIDTH), lambda b, j: (b, 0, 0))],
        out_specs=pl.BlockSpec((1, MOBA_BLOCK, ATTN_WIDTH), lambda b, j: (b, j, 0)),
        out_shape=jax.ShapeDtypeStruct((n, s, ATTN_WIDTH), F32),
        scratch_shapes=[pltpu.VMEM((npair, nb, 2 * MOBA_BLOCK), F32),
                        pltpu.VMEM((npair, PAIR, 2 * MOBA_BLOCK), BF16),
                        pltpu.VMEM((npair, 1, 2 * MOBA_BLOCK), F32),
                        pltpu.VMEM((npair, 1, 2 * MOBA_BLOCK), F32),
                        pltpu.VMEM((npair, PAIR, MOBA_BLOCK), F32)],
        compiler_params=_cparams(("arbitrary", "arbitrary")),
        name="moba_prompt",
    )(qt, kb, vt, kmean)


def _sublayer1_kernel(x_ref, pooled_ref, attn_ref, gp_ref, ga_ref, wg_ref, scale_ref, wpp_ref, wap_ref,
                      wout_ref, g1_ref, b1_ref, wr_ref, br_ref, h_ref, w4_ref, *, precise):
    tm = x_ref.shape[0]
    mm = _dot3 if precise else _dot_bf16
    pooled = pooled_ref[...]
    py = jnp.concatenate(
        [mm(pooled[:, g * POOL_GROUP_WIDTH:(g + 1) * POOL_GROUP_WIDTH], wg_ref[g])
         for g in range(len(POOL_WINDOWS))], axis=1) * scale_ref[...]
    a = mm(py, wpp_ref[...])
    b = mm(attn_ref[...], wap_ref[...])
    merged = gp_ref[...] * a + ga_ref[...] * b
    h = _layer_norm(ALPHA * x_ref[...] + mm(merged, wout_ref[...]), g1_ref[...], b1_ref[...])
    h_ref[...] = h

    lt = (_dot3(h, wr_ref[...]) + br_ref[...]).T
    idx = lax.broadcasted_iota(jnp.int32, (SUBLANES, tm), 0)
    idf = idx.astype(F32)
    gl = jnp.where(idx < N_GROUPS, lt[:SUBLANES], -jnp.inf)
    gmax, gsel = _first_argmax(gl, idf, 0, SUBLANES)
    gexp = jnp.exp(gl - gmax)
    gprob = gexp / jnp.sum(gexp, axis=0, keepdims=True)
    pg = jnp.sum(jnp.where(idf == gsel, gprob, 0.0), axis=0, keepdims=True)
    elog = jnp.zeros((SUBLANES, tm), F32)
    for g in range(N_GROUPS):
        r0 = EXPERT_ROW0 + g * EXPERTS_PER_GROUP
        elog = jnp.where(gsel == float(g), lt[r0:r0 + EXPERTS_PER_GROUP], elog)
    e1, i1 = _first_argmax(elog, idf, 0, SUBLANES)
    e2, i2 = _first_argmax(jnp.where(idf == i1, -jnp.inf, elog), idf, 0, SUBLANES)
    x2 = jnp.exp(e2 - e1)
    den = 1.0 + x2
    w_e = jnp.where(idf == i1, 1.0 / den, 0.0) + jnp.where(idf == i2, x2 / den, 0.0)
    pw = pg * w_e
    slabs = [jnp.where(gsel == float(g), pw, 0.0) for g in range(N_GROUPS)]
    slabs.append(jnp.zeros((ROUTER_LANES - N_GROUPS * EXPERTS_PER_GROUP, tm), F32))
    wn = jnp.concatenate(slabs, axis=0).T
    for g in range(N_GROUPS):
        sh = (ROUTER_LANES - g * EXPERTS_PER_GROUP) % ROUTER_LANES
        w4_ref[g] = wn if sh == 0 else pltpu.roll(wn, sh, 1)


def _sublayer1(x, pooled, attn, gp, ga, wts, tm, precise):
    t, d = x.shape
    row = lambda c: pl.BlockSpec((tm, c), lambda i: (i, 0))
    full = lambda a: pl.BlockSpec(a.shape, lambda i: (0,) * a.ndim)
    return pl.pallas_call(
        functools.partial(_sublayer1_kernel, precise=precise),
        grid=(t // tm,),
        in_specs=[row(d), row(POOL_WIDTH), row(ATTN_WIDTH), row(d), row(d)] + [full(a) for a in wts],
        out_specs=[row(d), pl.BlockSpec((N_GROUPS, tm, ROUTER_LANES), lambda i: (0, i, 0))],
        out_shape=[jax.ShapeDtypeStruct((t, d), F32),
                   jax.ShapeDtypeStruct((N_GROUPS, t, ROUTER_LANES), F32)],
        compiler_params=_cparams(("arbitrary",)),
        name="sublayer1",
    )(x, pooled, attn, gp, ga, *wts)


def _moe_kernel(h_ref, w_ref, weg_ref, weu_ref, wed_ref, g2_ref, b2_ref, y_ref, acc_ref):
    g = pl.program_id(1)

    @pl.when(g == 0)
    def _():
        acc_ref[...] = jnp.zeros_like(acc_ref)

    h = h_ref[...]
    hb = h.astype(BF16)
    w = w_ref[0]
    hid = []
    for e in range(EXPERTS_PER_GROUP):
        a = _dot(hb, weg_ref[0, e])
        b = _dot(hb, weu_ref[0, e])
        hid.append((a * _sigmoid(a) * b * w[:, e:e + 1]).astype(BF16))
    acc_ref[...] += _dot(jnp.concatenate(hid, axis=1), wed_ref[0])

    @pl.when(g == pl.num_programs(1) - 1)
    def _():
        y_ref[...] = _layer_norm(ALPHA * h + acc_ref[...], g2_ref[...], b2_ref[...])


def _moe(h, w4, weg, weu, wed, g2, b2, tm):
    t, d = h.shape
    ng, ne, _, f = weg.shape
    return pl.pallas_call(
        _moe_kernel,
        grid=(t // tm, ng),
        in_specs=[pl.BlockSpec((tm, d), lambda i, g: (i, 0)),
                  pl.BlockSpec((1, tm, ROUTER_LANES), lambda i, g: (g, i, 0)),
                  pl.BlockSpec((1, ne, d, f), lambda i, g: (g, 0, 0, 0)),
                  pl.BlockSpec((1, ne, d, f), lambda i, g: (g, 0, 0, 0)),
                  pl.BlockSpec((1, ne * f, d), lambda i, g: (g, 0, 0)),
                  pl.BlockSpec(g2.shape, lambda i, g: (0, 0)),
                  pl.BlockSpec(b2.shape, lambda i, g: (0, 0))],
        out_specs=pl.BlockSpec((tm, d), lambda i, g: (i, 0)),
        out_shape=jax.ShapeDtypeStruct((t, d), F32),
        scratch_shapes=[pltpu.VMEM((tm, d), F32)],
        compiler_params=_cparams(("arbitrary", "arbitrary")),
        name="moe",
    )(h, w4, weg, weu, wed, g2, b2)


def _lane_rep(row):
    return jnp.broadcast_to(row, (LANES, ATTN_WIDTH)).T.reshape(N_HEADS, HEAD_DIM, LANES)


def _sample_scores_kernel(pt_ref, q_ref, kn_ref, ck_ref, p_ref, idx_ref, own_ref,
                          buf_ref, lg_ref, sem_ref, *, n_pages, k_eff):
    n = pl.program_id(0)
    nseq = pl.num_programs(0)
    ppb = MOBA_BLOCK // PAGE_SIZE
    nblk = n_pages // ppb

    def page_copy(page, slot):
        return pltpu.make_async_copy(ck_ref.at[page], buf_ref.at[slot], sem_ref.at[slot])

    @pl.when(n == 0)
    def _():
        for s in range(PAGE_RING):
            page_copy(pt_ref[0, s], s).start()

    qrep = _lane_rep(q_ref[0])
    for i in range(n_pages):
        slot = i % PAGE_RING
        page_copy(pt_ref[n, i], slot).wait()
        lg = jnp.sum(buf_ref[slot] * qrep, axis=1)
        nxt = i + PAGE_RING
        if nxt < n_pages:
            page_copy(pt_ref[n, nxt], slot).start()
        else:
            @pl.when(n + 1 < nseq)
            def _():
                page_copy(pt_ref[n + 1, nxt - n_pages], slot).start()
        c0 = (i % ppb) * PAGE_SIZE
        lg_ref[i // ppb, :, c0:c0 + PAGE_SIZE] = lg

    logits = lg_ref[...]
    work = jnp.sum(logits, axis=2, keepdims=True) * (1.0 / MOBA_BLOCK)
    bf = lax.broadcasted_iota(jnp.int32, work.shape, 0).astype(F32)
    sel = jnp.zeros(work.shape, jnp.bool_)
    for k in range(k_eff):
        _, first = _first_argmax(work, bf, 0, nblk)
        idx_ref[0, k] = jnp.broadcast_to(first[0], (N_HEADS, LANES)).astype(jnp.int32)
        pick = bf == first
        sel = jnp.logical_or(sel, pick)
        work = jnp.where(pick, -jnp.inf, work)

    s_own = jnp.sum(_lane_rep(kn_ref[0]) * qrep, axis=1)[None, :, :1]
    lm = jnp.where(sel, logits, NEG_INF)
    m = jnp.maximum(jnp.max(jnp.max(lm, axis=2, keepdims=True), axis=0, keepdims=True), s_own)
    p = jnp.exp(lm - m)
    p_own = jnp.exp(s_own - m)
    l = jnp.sum(jnp.sum(p, axis=2, keepdims=True), axis=0, keepdims=True) + p_own
    p_ref[0] = p / l
    own_ref[0] = jnp.broadcast_to((p_own / l)[0], (N_HEADS, LANES))


def _sample_scores(page_table, q3, kn3, cache_kt, k_eff):
    nseq, n_pages = page_table.shape
    nblk = n_pages * PAGE_SIZE // MOBA_BLOCK
    row_spec = pl.BlockSpec((1, 1, ATTN_WIDTH), lambda n, pt: (n, 0, 0))
    return pl.pallas_call(
        functools.partial(_sample_scores_kernel, n_pages=n_pages, k_eff=k_eff),
        grid_spec=pltpu.PrefetchScalarGridSpec(
            num_scalar_prefetch=1,
            grid=(nseq,),
            in_specs=[row_spec, row_spec, pl.BlockSpec(memory_space=pl.ANY)],
            out_specs=[pl.BlockSpec((1, nblk, N_HEADS, MOBA_BLOCK), lambda n, pt: (n, 0, 0, 0)),
                       pl.BlockSpec((1, k_eff, N_HEADS, LANES), lambda n, pt: (n, 0, 0, 0)),
                       pl.BlockSpec((1, N_HEADS, LANES), lambda n, pt: (n, 0, 0))],
            scratch_shapes=[pltpu.VMEM((PAGE_RING, N_HEADS, HEAD_DIM, PAGE_SIZE), F32),
                            pltpu.VMEM((nblk, N_HEADS, MOBA_BLOCK), F32),
                            pltpu.SemaphoreType.DMA((PAGE_RING,))]),
        out_shape=[jax.ShapeDtypeStruct((nseq, nblk, N_HEADS, MOBA_BLOCK), F32),
                   jax.ShapeDtypeStruct((nseq, k_eff, N_HEADS, LANES), jnp.int32),
                   jax.ShapeDtypeStruct((nseq, N_HEADS, LANES), F32)],
        compiler_params=_cparams(("arbitrary",)),
        name="sample_scores",
    )(page_table, q3, kn3, cache_kt)


def _sample_pv_kernel(pt_ref, idx_ref, p_ref, own_ref, vn_ref, cv_ref, out_ref,
                      vbuf_ref, sem_ref, *, k_eff):
    n = pl.program_id(0)
    nseq = pl.num_programs(0)
    ppb = MOBA_BLOCK // PAGE_SIZE

    def copies(seq, slot):
        out = []
        for h in range(N_HEADS):
            for k in range(k_eff):
                blk = idx_ref[seq, k, h]
                for j in range(ppb):
                    page = pt_ref[seq, blk * ppb + j]
                    out.append(pltpu.make_async_copy(
                        cv_ref.at[page, h], vbuf_ref.at[slot, h, k * ppb + j], sem_ref.at[slot]))
        return out

    slot = n % 2

    @pl.when(n == 0)
    def _():
        for c in copies(0, 0):
            c.start()

    @pl.when(n + 1 < nseq)
    def _():
        for c in copies(n + 1, 1 - slot):
            c.start()

    for c in copies(n, slot):
        c.wait()

    accs = []
    for h in range(N_HEADS):
        acc = jnp.zeros((HEAD_DIM, PAGE_SIZE), F32)
        for k in range(k_eff):
            pk = p_ref[0, idx_ref[n, k, h], h:h + 1, :]
            for j in range(ppb):
                acc = acc + pk[:, j * PAGE_SIZE:(j + 1) * PAGE_SIZE] * vbuf_ref[slot, h, k * ppb + j]
        accs.append(acc)
    own = own_ref[0]
    accs += [jnp.broadcast_to(own[h:h + 1], (HEAD_DIM, LANES)) for h in range(N_HEADS)]
    r = _lane_sums_as_rows(jnp.concatenate(accs, axis=0))
    out_ref[0] = r[0:1, :ATTN_WIDTH] + r[0:1, ATTN_WIDTH:] * (1.0 / LANES) * vn_ref[0]


def _sample_pv(page_table, idx, p, own, vn3, cache_vt, k_eff):
    nseq = vn3.shape[0]
    nblk = p.shape[1]
    ppb = MOBA_BLOCK // PAGE_SIZE
    row_spec = pl.BlockSpec((1, 1, ATTN_WIDTH), lambda n, pt, ix: (n, 0, 0))
    return pl.pallas_call(
        functools.partial(_sample_pv_kernel, k_eff=k_eff),
        grid_spec=pltpu.PrefetchScalarGridSpec(
            num_scalar_prefetch=2,
            grid=(nseq,),
            in_specs=[pl.BlockSpec((1, nblk, N_HEADS, MOBA_BLOCK), lambda n, pt, ix: (n, 0, 0, 0)),
                      pl.BlockSpec((1, N_HEADS, LANES), lambda n, pt, ix: (n, 0, 0)),
                      row_spec, pl.BlockSpec(memory_space=pl.ANY)],
            out_specs=row_spec,
            scratch_shapes=[pltpu.VMEM((2, N_HEADS, k_eff * ppb, HEAD_DIM, PAGE_SIZE), F32),
                            pltpu.SemaphoreType.DMA((2,))]),
        out_shape=jax.ShapeDtypeStruct((nseq, 1, ATTN_WIDTH), F32),
        compiler_params=_cparams(("arbitrary",)),
        name="sample_pv",
    )(page_table, idx, p, own, vn3, cache_vt)


def kernel(x_prompt, x_sample, cache_k, cache_v, state_pool, page_table, w_in, w_pool_group,
           pool_scale, w_pool_proj, w_attn_proj, w_out, ln1_g, ln1_b, w_router_group,
           b_router_group, w_router_expert, b_router_expert, w_exp_gate, w_exp_up,
           w_exp_down, ln2_g, ln2_b):
    assert w_in.shape[0] == DEPTH == 1
    n, s, d = x_prompt.shape
    nd, ld, _ = x_sample.shape
    assert ld == 1
    past_len = page_table.shape[1] * PAGE_SIZE
    assert past_len % MOBA_BLOCK == 0 and past_len // MOBA_BLOCK >= MOBA_TOPK
    layer = 0

    a0 = POOL_WIDTH
    win = w_in[layer]
    wn_f = jnp.concatenate([win[:, :a0], win[:, a0 + 2 * ATTN_WIDTH:]], axis=1)
    wt_f = win[:, a0:a0 + 2 * ATTN_WIDTH].T
    wn, wt = wn_f.astype(BF16), wt_f.astype(BF16)
    half = ROT_DIM // 2
    inv = (ROPE_THETA ** (-2.0 * jnp.arange(half, dtype=F32) / ROT_DIM)).reshape(half, 1)
    row = lambda a: a.reshape(1, -1)
    ng, ne = N_GROUPS, EXPERTS_PER_GROUP
    wr = jnp.zeros((d, ROUTER_LANES), F32)
    wr = wr.at[:, :ng].set(w_router_group[layer])
    wr = wr.at[:, EXPERT_ROW0:EXPERT_ROW0 + ng * ne].set(
        jnp.transpose(w_router_expert[layer], (1, 0, 2)).reshape(d, ng * ne))
    br = jnp.zeros((1, ROUTER_LANES), F32)
    br = br.at[0, :ng].set(b_router_group[layer])
    br = br.at[0, EXPERT_ROW0:EXPERT_ROW0 + ng * ne].set(b_router_expert[layer].reshape(-1))
    sub_f = (w_pool_group[layer], row(pool_scale[layer]), w_pool_proj[layer], w_attn_proj[layer],
             w_out[layer], row(ln1_g[layer]), row(ln1_b[layer]), wr, br)
    cast = (0, 2, 3, 4)
    sub_b = tuple(a.astype(BF16) if i in cast else a for i, a in enumerate(sub_f))
    weg = w_exp_gate[layer].astype(BF16)
    weu = w_exp_up[layer].astype(BF16)
    f = weg.shape[-1]
    wed = w_exp_down[layer].astype(BF16).reshape(ng, ne * f, d)
    g2, b2 = row(ln2_g[layer]), row(ln2_b[layer])

    tm = 512
    pooled, qt, k_p, kb, kmean, v_p, vt, gp, ga, utail = _proj_prompt(x_prompt, wn, wt, inv, tm)
    attn = _moba_prompt(qt, kb, vt, kmean)
    t = n * s
    flat = lambda a: a.reshape(t, a.shape[-1])
    h_p, w4_p = _sublayer1(flat(x_prompt), flat(pooled), flat(attn), flat(gp), flat(ga), sub_b, tm, False)
    y_p = _moe(h_p, w4_p, weg, weu, wed, g2, b2, tm).reshape(n, s, d)

    xs = x_sample.reshape(nd, d)
    state_t = jnp.transpose(state_pool[layer], (1, 0, 2))
    pooled_s, u_s, q_s, k_s, v_s, gp_s, ga_s = _proj_sample(xs, wn_f, wt_f, inv, state_t, past_len)
    cache_kt = jnp.transpose(cache_k[layer], (0, 2, 3, 1))
    cache_vt = jnp.transpose(cache_v[layer], (0, 2, 3, 1))
    r3 = lambda a: a.reshape(nd, 1, a.shape[-1])
    k_eff = MOBA_TOPK
    p_sel, idx, own = _sample_scores(page_table, r3(q_s), r3(k_s), cache_kt, k_eff)
    attn_s = _sample_pv(page_table, idx[:, :, :, 0], p_sel, own, r3(v_s), cache_vt,
                        k_eff).reshape(nd, ATTN_WIDTH)
    h_s, w4_s = _sublayer1(xs, pooled_s, attn_s, gp_s, ga_s, sub_f, nd, True)
    y_s = _moe(h_s, w4_s, weg, weu, wed, g2, b2, nd).reshape(nd, 1, d)

    heads = lambda a, lead: a.reshape(lead + (N_HEADS, HEAD_DIM))
    pool_prompt = utail[:, HALO_ROWS - POOL_BUF:][None]
    pool_sample = jnp.concatenate([state_pool[layer][:, 1:], u_s[:, None, :]], axis=1)[None]
    return (y_p, y_s.reshape(nd, ld, d),
            heads(k_p, (1, n, s)), heads(v_p, (1, n, s)), pool_prompt,
            heads(k_s, (1, nd, 1)), heads(v_s, (1, nd, 1)), pool_sample)
```

```python
import functools

import jax
import jax.numpy as jnp
from jax import lax
from jax.experimental import pallas as pl
from jax.experimental.pallas import tpu as pltpu

F32 = jnp.float32
BF16 = jnp.bfloat16

POOL_WINDOWS = (2, 4, 8, 16)
POOL_GROUP_WIDTH = 128
POOL_WIDTH = 512
POOL_BUF = 15
HEAD_DIM = 64
N_HEADS = 8
ATTN_WIDTH = 512
ROT_DIM = 16
ROPE_THETA = 500000.0
MOBA_BLOCK = 256
MOBA_TOPK = 3
PAGE_SIZE = 128
N_GROUPS = 4
EXPERTS_PER_GROUP = 8
EXPERT_TOPK = 2
DEPTH = 1
ALPHA = (2 * DEPTH) ** 0.25
LN_EPS = 1e-5
NEG_INF = -1e30

LANES = 128
SUBLANES = 8
HALO_ROWS = 16
VMEM_LIMIT = 56 * 1024 * 1024

ROUTER_LANES = 128
EXPERT_ROW0 = 8
PAIR = 2 * HEAD_DIM
PAGE_RING = 32


def _cparams(sem):
    return pltpu.CompilerParams(dimension_semantics=sem, vmem_limit_bytes=VMEM_LIMIT)


def _dot(a, b):
    return jnp.dot(a, b, preferred_element_type=F32)


def _dot_nt(a, b):
    return lax.dot_general(a, b, (((1,), (1,)), ((), ())), preferred_element_type=F32)


def _split_bf16(x):
    hi = x.astype(BF16)
    lo = (x - hi.astype(F32)).astype(BF16)
    return hi, lo


def _dot3(a, b, dot=_dot):
    ah, al = _split_bf16(a)
    bh, bl = _split_bf16(b)
    return dot(ah, bh) + dot(ah, bl) + dot(al, bh)


def _dot_bf16(a, b):
    return _dot(a.astype(BF16), b.astype(BF16))


def _lane_sums_as_rows(x):
    ones = jnp.ones((SUBLANES, LANES), BF16)
    hi, lo = _split_bf16(x)
    return _dot_nt(ones, hi) + _dot_nt(ones, lo)


def _sigmoid(x):
    return 1.0 / (1.0 + jnp.exp(-x))


def _layer_norm(x, g, b):
    mu = jnp.mean(x, axis=-1, keepdims=True)
    xc = x - mu
    var = jnp.mean(xc * xc, axis=-1, keepdims=True)
    return xc * lax.rsqrt(var + LN_EPS) * g + b


def _rotary_t(zt, cos, sin):
    half = ROT_DIM // 2
    parts = []
    for h in range(N_HEADS):
        r0 = h * HEAD_DIM
        x1 = zt[r0:r0 + half]
        x2 = zt[r0 + half:r0 + ROT_DIM]
        parts += [x1 * cos - x2 * sin, x2 * cos + x1 * sin, zt[r0 + ROT_DIM:r0 + HEAD_DIM]]
    return jnp.concatenate(parts, axis=0)


def _cos_sin(inv, pos_row):
    ang = inv * pos_row
    return jnp.cos(ang), jnp.sin(ang)


def _first_argmax(work, idxf, axis, n):
    mx = jnp.max(work, axis=axis, keepdims=True)
    first = jnp.min(jnp.where(work == mx, idxf, float(n)), axis=axis, keepdims=True)
    return mx, first


def _proj_prompt_kernel(x_ref, wn_ref, wt_ref, inv_ref,
                        pooled_ref, qt_ref, k_ref, kb_ref, kmean_ref, v_ref, vt_ref,
                        gp_ref, ga_ref, utail_ref, halo_ref, *, tm):
    i = pl.program_id(1)
    nblk = tm // MOBA_BLOCK

    @pl.when(i == 0)
    def _():
        halo_ref[...] = jnp.zeros_like(halo_ref)

    xb = x_ref[0].astype(BF16)
    zn = _dot(xb, wn_ref[...])
    zt = _dot_nt(wt_ref[...], xb)

    u = zn[:, :POOL_WIDTH]
    ext = jnp.concatenate([halo_ref[...], u], axis=0)
    pos1 = i * tm + lax.broadcasted_iota(jnp.int32, (tm, POOL_GROUP_WIDTH), 0) + 1
    outs = []
    for g, w in enumerate(POOL_WINDOWS):
        c0 = g * POOL_GROUP_WIDTH
        s = ext[:, c0:c0 + POOL_GROUP_WIDTH]
        k = 1
        while k < w:
            n = s.shape[0]
            s = s[k:] + s[:n - k]
            k *= 2
        s = s[HALO_ROWS + 1 - w:HALO_ROWS + 1 - w + tm]
        cnt = jnp.minimum(pos1, w).astype(F32)
        outs.append(s / cnt - u[:, c0:c0 + POOL_GROUP_WIDTH])
    pooled_ref[0] = jnp.concatenate(outs, axis=1).astype(BF16)
    halo_ref[...] = u[tm - HALO_ROWS:]
    utail_ref[0] = u[tm - HALO_ROWS:]

    vt_f = zn[:, POOL_WIDTH:POOL_WIDTH + ATTN_WIDTH].T
    v_ref[0] = vt_f
    vt = vt_f.astype(BF16)
    gates = _sigmoid(zn[:, POOL_WIDTH + ATTN_WIDTH:])
    d = gp_ref.shape[-1]
    gp_ref[0] = gates[:, :d]
    ga_ref[0] = gates[:, d:]

    pos = (i * tm + lax.broadcasted_iota(jnp.int32, (1, tm), 1)).astype(F32)
    cos, sin = _cos_sin(inv_ref[...], pos)
    qt = _rotary_t(zt[:ATTN_WIDTH], cos, sin) * (HEAD_DIM ** -0.5)
    kt_f = _rotary_t(zt[ATTN_WIDTH:], cos, sin)
    k_ref[0] = kt_f
    kn = kt_f.T
    kb_ref[0] = kn.astype(BF16)
    for j in range(nblk):
        sl = slice(j * MOBA_BLOCK, (j + 1) * MOBA_BLOCK)
        qt_ref[0, j] = qt[:, sl]
        vt_ref[0, j] = vt[:, sl]
        kmean_ref[0, pl.ds(i * nblk + j, 1), :] = (
            jnp.sum(kn[sl], axis=0, keepdims=True) * (1.0 / MOBA_BLOCK))


def _proj_prompt(x, wn, wt, inv, tm):
    n, s, d = x.shape
    nb = s // MOBA_BLOCK
    nblk = tm // MOBA_BLOCK
    tok = lambda c, dt: jax.ShapeDtypeStruct((n, s, c), dt)
    blk = lambda dt: jax.ShapeDtypeStruct((n, nb, ATTN_WIDTH, MOBA_BLOCK), dt)
    tok_spec = lambda c: pl.BlockSpec((1, tm, c), lambda b, i: (b, i, 0))
    blk_spec = pl.BlockSpec((1, nblk, ATTN_WIDTH, MOBA_BLOCK), lambda b, i: (b, i, 0, 0))
    full = lambda a: pl.BlockSpec(a.shape, lambda b, i: (0,) * a.ndim)
    tr = jax.ShapeDtypeStruct((n, ATTN_WIDTH, s), F32)
    tr_spec = pl.BlockSpec((1, ATTN_WIDTH, tm), lambda b, i: (b, 0, i))
    return pl.pallas_call(
        functools.partial(_proj_prompt_kernel, tm=tm),
        grid=(n, s // tm),
        in_specs=[tok_spec(d), full(wn), full(wt), full(inv)],
        out_specs=[tok_spec(POOL_WIDTH), blk_spec, tr_spec, tok_spec(ATTN_WIDTH),
                   pl.BlockSpec((1, nb, ATTN_WIDTH), lambda b, i: (b, 0, 0)),
                   tr_spec, blk_spec, tok_spec(d), tok_spec(d),
                   pl.BlockSpec((1, HALO_ROWS, POOL_WIDTH), lambda b, i: (b, 0, 0))],
        out_shape=[tok(POOL_WIDTH, BF16), blk(F32), tr, tok(ATTN_WIDTH, BF16),
                   jax.ShapeDtypeStruct((n, nb, ATTN_WIDTH), F32),
                   tr, blk(BF16), tok(d, F32), tok(d, F32),
                   jax.ShapeDtypeStruct((n, HALO_ROWS, POOL_WIDTH), F32)],
        scratch_shapes=[pltpu.VMEM((HALO_ROWS, POOL_WIDTH), F32)],
        compiler_params=_cparams(("arbitrary", "arbitrary")),
        name="proj_prompt",
    )(x, wn, wt, inv)


def _proj_sample_kernel(x_ref, wn_ref, wt_ref, inv_ref, st_ref,
                        pooled_ref, u_ref, q_ref, k_ref, v_ref, gp_ref, ga_ref, *, past_len):
    x = x_ref[...]
    uv_w = POOL_WIDTH + ATTN_WIDTH
    zn = _dot3(x, wn_ref[:, :uv_w])
    zt = _dot3(wt_ref[...], x, _dot_nt)
    m = x.shape[0]

    u = zn[:, :POOL_WIDTH]
    u_ref[...] = u
    outs = []
    for g, w in enumerate(POOL_WINDOWS):
        c0 = g * POOL_GROUP_WIDTH
        s = u[:, c0:c0 + POOL_GROUP_WIDTH]
        for r in range(POOL_BUF - (w - 1), POOL_BUF):
            s = s + st_ref[r, :, c0:c0 + POOL_GROUP_WIDTH]
        cnt = float(min(past_len + 1, w))
        outs.append(s / cnt - u[:, c0:c0 + POOL_GROUP_WIDTH])
    pooled_ref[...] = jnp.concatenate(outs, axis=1)

    v_ref[...] = zn[:, POOL_WIDTH:]
    gates = _sigmoid(_dot_bf16(x, wn_ref[:, uv_w:]))
    d = gp_ref.shape[-1]
    gp_ref[...] = gates[:, :d]
    ga_ref[...] = gates[:, d:]

    pos = jnp.full((1, m), float(past_len), F32)
    cos, sin = _cos_sin(inv_ref[...], pos)
    q_ref[...] = (_rotary_t(zt[:ATTN_WIDTH], cos, sin) * (HEAD_DIM ** -0.5)).T
    k_ref[...] = _rotary_t(zt[ATTN_WIDTH:], cos, sin).T


def _proj_sample(x, wn, wt, inv, state_t, past_len):
    m, d = x.shape
    o = lambda c: jax.ShapeDtypeStruct((m, c), F32)
    return pl.pallas_call(
        functools.partial(_proj_sample_kernel, past_len=past_len),
        out_shape=[o(POOL_WIDTH), o(POOL_WIDTH), o(ATTN_WIDTH), o(ATTN_WIDTH),
                   o(ATTN_WIDTH), o(d), o(d)],
        compiler_params=pltpu.CompilerParams(vmem_limit_bytes=VMEM_LIMIT),
        name="proj_sample",
    )(x, wn, wt, inv, state_t)


def _moba_prompt_kernel(qt_ref, kb_ref, vt_ref, kmean_ref, out_ref,
                        sel_ref, qbd_ref, m_ref, l_ref, acc_ref):
    blk = pl.program_id(1)
    nb = kmean_ref.shape[1]
    npair = ATTN_WIDTH // PAIR
    two = 2 * MOBA_BLOCK
    rows = lax.broadcasted_iota(jnp.int32, (PAIR, MOBA_BLOCK), 0)
    jj = lax.broadcasted_iota(jnp.int32, (nb, two), 0)
    jf = jj.astype(F32)
    valid = jj < blk

    for p in range(npair):
        ps = slice(p * PAIR, (p + 1) * PAIR)
        qt = qt_ref[0, 0, ps, :]
        qbd = jnp.concatenate([jnp.where(rows < HEAD_DIM, qt, 0.0),
                               jnp.where(rows >= HEAD_DIM, qt, 0.0)], axis=1)
        qbd_ref[p] = qbd.astype(BF16)
        work = jnp.where(valid, _dot3(kmean_ref[0, :, ps], qbd), NEG_INF)
        sel = jnp.zeros((nb, two), jnp.bool_)
        for _ in range(min(MOBA_TOPK, nb - 1)):
            _, first = _first_argmax(work, jf, 0, nb)
            pick = jf == first
            sel = jnp.logical_or(sel, pick)
            work = jnp.where(pick, -jnp.inf, work)
        sel_ref[p] = jnp.where(jnp.logical_and(sel, valid), 1.0, 0.0)
    for h in range(N_HEADS):
        m_ref[h] = jnp.full((1, MOBA_BLOCK), NEG_INF, F32)
        l_ref[h] = jnp.zeros((1, MOBA_BLOCK), F32)
        acc_ref[h] = jnp.zeros((HEAD_DIM, MOBA_BLOCK), F32)

    half = MOBA_BLOCK // 2
    kpos = lax.broadcasted_iota(jnp.int32, (half, MOBA_BLOCK), 0)
    qpos = lax.broadcasted_iota(jnp.int32, (half, MOBA_BLOCK), 1)

    def attend(j, past):
        row0 = pl.multiple_of(j * MOBA_BLOCK, MOBA_BLOCK)
        sts = []
        for h in range(N_HEADS):
            p, a = divmod(h, 2)
            qs = slice(a * MOBA_BLOCK, (a + 1) * MOBA_BLOCK)
            for kh in range(2):
                kt = kb_ref[0, pl.ds(row0 + kh * half, half), p * PAIR:(p + 1) * PAIR]
                st = _dot(kt, qbd_ref[p, :, qs])
                if not past:
                    st = jnp.where(kpos + kh * half <= qpos, st, NEG_INF)
                sts.append(st)
        for h in range(N_HEADS):
            p, a = divmod(h, 2)
            qs = slice(a * MOBA_BLOCK, (a + 1) * MOBA_BLOCK)
            st0, st1 = sts[2 * h], sts[2 * h + 1]
            m = m_ref[h]
            mj = jnp.maximum(jnp.max(st0, axis=0, keepdims=True), jnp.max(st1, axis=0, keepdims=True))
            if past:
                on = sel_ref[p, pl.ds(j, 1), qs]
                m_new = jnp.where(on > 0.0, jnp.maximum(m, mj), m)
                shift = jnp.where(on > 0.0, m_new, mj)
            else:
                m_new = jnp.maximum(m, mj)
                shift = m_new
            pr0 = jnp.exp(st0 - shift)
            pr1 = jnp.exp(st1 - shift)
            alpha = jnp.exp(m - m_new)
            psum = jnp.sum(pr0, axis=0, keepdims=True) + jnp.sum(pr1, axis=0, keepdims=True)
            hs = slice(h * HEAD_DIM, (h + 1) * HEAD_DIM)
            o = (_dot(vt_ref[0, j, hs, :half], pr0.astype(BF16))
                 + _dot(vt_ref[0, j, hs, half:], pr1.astype(BF16)))
            if past:
                psum, o = on * psum, on * o
            m_ref[h] = m_new
            l_ref[h] = alpha * l_ref[h] + psum
            acc_ref[h] = alpha * acc_ref[h] + o

    def past_block(j, carry):
        attend(j, True)
        return carry

    lax.fori_loop(0, blk, past_block, 0)
    attend(blk, False)

    for p in range(npair):
        ha, hb = 2 * p, 2 * p + 1
        o = jnp.concatenate([acc_ref[ha] / l_ref[ha], acc_ref[hb] / l_ref[hb]], axis=0)
        out_ref[0, :, p * PAIR:(p + 1) * PAIR] = o.T


def _moba_prompt(qt, kb, vt, kmean):
    n, nb, _, _ = qt.shape
    s = kb.shape[1]
    npair = ATTN_WIDTH // PAIR
    return pl.pallas_call(
        _moba_prompt_kernel,
        grid=(n, nb),
        in_specs=[pl.BlockSpec((1, 1, ATTN_WIDTH, MOBA_BLOCK), lambda b, j: (b, j, 0, 0)),
                  pl.BlockSpec((1, s, ATTN_WIDTH), lambda b, j: (b, 0, 0)),
                  pl.BlockSpec((1, nb, ATTN_WIDTH, MOBA_BLOCK), lambda b, j: (b, 0, 0, 0)),
                  pl.BlockSpec((1, nb, ATTN_WIDTH), lambda b, j: (b, 0, 0))],
        out_specs=pl.BlockSpec((1, MOBA_BLOCK, ATTN_WIDTH), lambda b, j: (b, j, 0)),
        out_shape=jax.ShapeDtypeStruct((n, s, ATTN_WIDTH), F32),
        scratch_shapes=[pltpu.VMEM((npair, nb, 2 * MOBA_BLOCK), F32),
                        pltpu.VMEM((npair, PAIR, 2 * MOBA_BLOCK), BF16),
                        pltpu.VMEM((N_HEADS, 1, MOBA_BLOCK), F32),
                        pltpu.VMEM((N_HEADS, 1, MOBA_BLOCK), F32),
                        pltpu.VMEM((N_HEADS, HEAD_DIM, MOBA_BLOCK), F32)],
        compiler_params=_cparams(("arbitrary", "arbitrary")),
        name="moba_prompt",
    )(qt, kb, vt, kmean)


def _sublayer1_kernel(x_ref, pooled_ref, attn_ref, gp_ref, ga_ref, wg_ref, scale_ref, wpp_ref, wap_ref,
                      wout_ref, g1_ref, b1_ref, wr_ref, br_ref, h_ref, w4_ref, gsel_ref, *, precise):
    tm = x_ref.shape[0]
    mm = _dot3 if precise else _dot_bf16
    pooled = pooled_ref[...]
    py = jnp.concatenate(
        [mm(pooled[:, g * POOL_GROUP_WIDTH:(g + 1) * POOL_GROUP_WIDTH], wg_ref[g])
         for g in range(len(POOL_WINDOWS))], axis=1) * scale_ref[...]
    a = mm(py, wpp_ref[...])
    b = mm(attn_ref[...], wap_ref[...])
    merged = gp_ref[...] * a + ga_ref[...] * b
    h = _layer_norm(ALPHA * x_ref[...] + mm(merged, wout_ref[...]), g1_ref[...], b1_ref[...])
    h_ref[...] = h

    lt = (_dot3(h, wr_ref[...]) + br_ref[...]).T
    idx = lax.broadcasted_iota(jnp.int32, (SUBLANES, tm), 0)
    idf = idx.astype(F32)
    gl = jnp.where(idx < N_GROUPS, lt[:SUBLANES], -jnp.inf)
    gmax, gsel = _first_argmax(gl, idf, 0, SUBLANES)
    gsel_ref[0] = gsel
    gexp = jnp.exp(gl - gmax)
    gprob = gexp / jnp.sum(gexp, axis=0, keepdims=True)
    pg = jnp.sum(jnp.where(idf == gsel, gprob, 0.0), axis=0, keepdims=True)
    elog = jnp.zeros((SUBLANES, tm), F32)
    for g in range(N_GROUPS):
        r0 = EXPERT_ROW0 + g * EXPERTS_PER_GROUP
        elog = jnp.where(gsel == float(g), lt[r0:r0 + EXPERTS_PER_GROUP], elog)
    e1, i1 = _first_argmax(elog, idf, 0, SUBLANES)
    e2, i2 = _first_argmax(jnp.where(idf == i1, -jnp.inf, elog), idf, 0, SUBLANES)
    x2 = jnp.exp(e2 - e1)
    den = 1.0 + x2
    w_e = jnp.where(idf == i1, 1.0 / den, 0.0) + jnp.where(idf == i2, x2 / den, 0.0)
    pw = pg * w_e
    slabs = [jnp.where(gsel == float(g), pw, 0.0) for g in range(N_GROUPS)]
    slabs.append(jnp.zeros((ROUTER_LANES - N_GROUPS * EXPERTS_PER_GROUP, tm), F32))
    wn = jnp.concatenate(slabs, axis=0).T
    for g in range(N_GROUPS):
        sh = (ROUTER_LANES - g * EXPERTS_PER_GROUP) % ROUTER_LANES
        w4_ref[g] = wn if sh == 0 else pltpu.roll(wn, sh, 1)


def _sublayer1(x, pooled, attn, gp, ga, wts, tm, precise):
    t, d = x.shape
    row = lambda c: pl.BlockSpec((tm, c), lambda i: (i, 0))
    full = lambda a: pl.BlockSpec(a.shape, lambda i: (0,) * a.ndim)
    return pl.pallas_call(
        functools.partial(_sublayer1_kernel, precise=precise),
        grid=(t // tm,),
        in_specs=[row(d), row(POOL_WIDTH), row(ATTN_WIDTH), row(d), row(d)] + [full(a) for a in wts],
        out_specs=[row(d), pl.BlockSpec((N_GROUPS, tm, ROUTER_LANES), lambda i: (0, i, 0)),
                   pl.BlockSpec((1, 1, tm), lambda i: (i, 0, 0))],
        out_shape=[jax.ShapeDtypeStruct((t, d), F32),
                   jax.ShapeDtypeStruct((N_GROUPS, t, ROUTER_LANES), F32),
                   jax.ShapeDtypeStruct((t // tm, 1, tm), F32)],
        compiler_params=_cparams(("arbitrary",)),
        name="sublayer1",
    )(x, pooled, attn, gp, ga, *wts)


def _moe_kernel(h_ref, w_ref, weg_ref, weu_ref, wed_ref, g2_ref, b2_ref, y_ref, acc_ref):
    g = pl.program_id(1)

    @pl.when(g == 0)
    def _():
        acc_ref[...] = jnp.zeros_like(acc_ref)

    h = h_ref[...]
    hb = h.astype(BF16)
    w = w_ref[0]
    hid = []
    for e in range(EXPERTS_PER_GROUP):
        a = _dot(hb, weg_ref[0, e])
        b = _dot(hb, weu_ref[0, e])
        hid.append((a * _sigmoid(a) * b * w[:, e:e + 1]).astype(BF16))
    acc_ref[...] += _dot(jnp.concatenate(hid, axis=1), wed_ref[0])

    @pl.when(g == pl.num_programs(1) - 1)
    def _():
        y_ref[...] = _layer_norm(ALPHA * h + acc_ref[...], g2_ref[...], b2_ref[...])


def _moe(h, w4, weg, weu, wed, g2, b2, tm):
    t, d = h.shape
    ng, ne, _, f = weg.shape
    return pl.pallas_call(
        _moe_kernel,
        grid=(t // tm, ng),
        in_specs=[pl.BlockSpec((tm, d), lambda i, g: (i, 0)),
                  pl.BlockSpec((1, tm, ROUTER_LANES), lambda i, g: (g, i, 0)),
                  pl.BlockSpec((1, ne, d, f), lambda i, g: (g, 0, 0, 0)),
                  pl.BlockSpec((1, ne, d, f), lambda i, g: (g, 0, 0, 0)),
                  pl.BlockSpec((1, ne * f, d), lambda i, g: (g, 0, 0)),
                  pl.BlockSpec(g2.shape, lambda i, g: (0, 0)),
                  pl.BlockSpec(b2.shape, lambda i, g: (0, 0))],
        out_specs=pl.BlockSpec((tm, d), lambda i, g: (i, 0)),
        out_shape=jax.ShapeDtypeStruct((t, d), F32),
        scratch_shapes=[pltpu.VMEM((tm, d), F32)],
        compiler_params=_cparams(("arbitrary", "arbitrary")),
        name="moe",
    )(h, w4, weg, weu, wed, g2, b2)


MOE_CAP = 384


def _lane_prefix_sum(x):
    n = x.shape[1]
    lane = lax.broadcasted_iota(jnp.int32, x.shape, 1)
    s = 1
    while s < n:
        x = x + jnp.where(lane >= s, pltpu.roll(x, s, 1), 0.0)
        s *= 2
    return x


def _moe_grouped_kernel(h_ref, gsel_ref, w_ref, weg_ref, weu_ref, wed_ref, g2_ref, b2_ref, y_ref,
                        hb_ref, acc_ref, rank_ref, rankt_ref):
    g = pl.program_id(1)
    tm = h_ref.shape[0]
    cap = MOE_CAP

    @pl.when(g == 0)
    def _():
        hb_ref[...] = h_ref[...].astype(BF16)
        acc_ref[...] = jnp.zeros_like(acc_ref)
        sub = lax.broadcasted_iota(jnp.int32, (SUBLANES, tm), 0).astype(F32)
        member = jnp.where(sub == gsel_ref[0], 1.0, 0.0)
        rank = jnp.where(member > 0.0, _lane_prefix_sum(member) - 1.0, -1.0)
        rank_ref[...] = rank
        rank_t = jnp.concatenate([rank, jnp.full((LANES - SUBLANES, tm), -1.0, F32)], axis=0).T
        for gg in range(N_GROUPS):
            rankt_ref[gg] = jnp.broadcast_to(rank_t[:, gg:gg + 1], (tm, LANES))

    rrow = rank_ref[pl.ds(g, 1), :]
    rcol = rankt_ref[g]
    wh, wl = _split_bf16(w_ref[0])
    count = jnp.max(rrow).astype(jnp.int32) + 1
    n_pass = (count + (cap - 1)) // cap
    sub_i = lax.broadcasted_iota(jnp.int32, (cap, tm), 0).astype(F32)
    lane_i = lax.broadcasted_iota(jnp.int32, (tm, LANES), 1).astype(F32)

    def one_pass(ps, carry):
        base = (ps * cap).astype(F32)
        pick = jnp.where(rrow - base == sub_i, 1.0, 0.0).astype(BF16)
        xg = _dot(pick, hb_ref[...]).astype(BF16)
        wg = _dot(pick, wh) + _dot(pick, wl)
        hid = []
        for e in range(EXPERTS_PER_GROUP):
            a = _dot(xg, weg_ref[0, e])
            b = _dot(xg, weu_ref[0, e])
            hid.append((a * _sigmoid(a) * b * wg[:, e:e + 1]).astype(BF16))
        yg = _dot(jnp.concatenate(hid, axis=1), wed_ref[0]).astype(BF16)
        put = jnp.concatenate(
            [jnp.where(rcol - base == lane_i + float(k * LANES), 1.0, 0.0) for k in range(cap // LANES)],
            axis=1).astype(BF16)
        acc_ref[...] += _dot(put, yg)
        return carry

    lax.fori_loop(0, n_pass, one_pass, 0)

    @pl.when(g == pl.num_programs(1) - 1)
    def _():
        y_ref[...] = _layer_norm(ALPHA * h_ref[...] + acc_ref[...], g2_ref[...], b2_ref[...])


def _moe_grouped(h, gsel, w4, weg, weu, wed, g2, b2, tm):
    t, d = h.shape
    ng, ne, _, f = weg.shape
    gsel = gsel.reshape(t // tm, 1, tm)
    return pl.pallas_call(
        _moe_grouped_kernel,
        grid=(t // tm, ng),
        in_specs=[pl.BlockSpec((tm, d), lambda i, g: (i, 0)),
                  pl.BlockSpec((1, 1, tm), lambda i, g: (i, 0, 0)),
                  pl.BlockSpec((1, tm, ROUTER_LANES), lambda i, g: (g, i, 0)),
                  pl.BlockSpec((1, ne, d, f), lambda i, g: (g, 0, 0, 0)),
                  pl.BlockSpec((1, ne, d, f), lambda i, g: (g, 0, 0, 0)),
                  pl.BlockSpec((1, ne * f, d), lambda i, g: (g, 0, 0)),
                  pl.BlockSpec(g2.shape, lambda i, g: (0, 0)),
                  pl.BlockSpec(b2.shape, lambda i, g: (0, 0))],
        out_specs=pl.BlockSpec((tm, d), lambda i, g: (i, 0)),
        out_shape=jax.ShapeDtypeStruct((t, d), F32),
        scratch_shapes=[pltpu.VMEM((tm, d), BF16), pltpu.VMEM((tm, d), F32),
                        pltpu.VMEM((SUBLANES, tm), F32), pltpu.VMEM((N_GROUPS, tm, LANES), F32)],
        compiler_params=_cparams(("arbitrary", "arbitrary")),
        name="moe_grouped",
    )(h, gsel, w4, weg, weu, wed, g2, b2)


def _lane_rep(row):
    return jnp.broadcast_to(row, (LANES, ATTN_WIDTH)).T.reshape(N_HEADS, HEAD_DIM, LANES)


def _sample_scores_kernel(pt_ref, q_ref, kn_ref, ck_ref, p_ref, idx_ref, own_ref,
                          buf_ref, lg_ref, sem_ref, *, n_pages, k_eff):
    n = pl.program_id(0)
    nseq = pl.num_programs(0)
    ppb = MOBA_BLOCK // PAGE_SIZE
    nblk = n_pages // ppb

    def page_copy(page, slot):
        return pltpu.make_async_copy(ck_ref.at[page], buf_ref.at[slot], sem_ref.at[slot])

    ring = buf_ref.shape[0]

    @pl.when(n == 0)
    def _():
        for s in range(ring):
            page_copy(pt_ref[0, s], s).start()

    qrep = _lane_rep(q_ref[0])
    for i in range(n_pages):
        slot = i % ring
        page_copy(pt_ref[n, i], slot).wait()
        lg = jnp.sum(buf_ref[slot] * qrep, axis=1)
        nxt = i + ring
        if nxt < n_pages:
            page_copy(pt_ref[n, nxt], slot).start()
        else:
            @pl.when(n + 1 < nseq)
            def _():
                page_copy(pt_ref[n + 1, nxt - n_pages], slot).start()
        c0 = (i % ppb) * PAGE_SIZE
        lg_ref[i // ppb, :, c0:c0 + PAGE_SIZE] = lg

    logits = lg_ref[...]
    work = jnp.sum(logits, axis=2, keepdims=True) * (1.0 / MOBA_BLOCK)
    bf = lax.broadcasted_iota(jnp.int32, work.shape, 0).astype(F32)
    sel = jnp.zeros(work.shape, jnp.bool_)
    for k in range(k_eff):
        _, first = _first_argmax(work, bf, 0, nblk)
        idx_ref[0, k] = jnp.broadcast_to(first[0], (N_HEADS, LANES)).astype(jnp.int32)
        pick = bf == first
        sel = jnp.logical_or(sel, pick)
        work = jnp.where(pick, -jnp.inf, work)

    s_own = jnp.sum(_lane_rep(kn_ref[0]) * qrep, axis=1)[None, :, :1]
    lm = jnp.where(sel, logits, NEG_INF)
    m = jnp.maximum(jnp.max(jnp.max(lm, axis=2, keepdims=True), axis=0, keepdims=True), s_own)
    p = jnp.exp(lm - m)
    p_own = jnp.exp(s_own - m)
    l = jnp.sum(jnp.sum(p, axis=2, keepdims=True), axis=0, keepdims=True) + p_own
    p_ref[0] = p / l
    own_ref[0] = jnp.broadcast_to((p_own / l)[0], (N_HEADS, LANES))


def _sample_scores(page_table, q3, kn3, cache_kt, k_eff):
    nseq, n_pages = page_table.shape
    nblk = n_pages * PAGE_SIZE // MOBA_BLOCK
    ring = min(PAGE_RING, n_pages)
    row_spec = pl.BlockSpec((1, 1, ATTN_WIDTH), lambda n, pt: (n, 0, 0))
    return pl.pallas_call(
        functools.partial(_sample_scores_kernel, n_pages=n_pages, k_eff=k_eff),
        grid_spec=pltpu.PrefetchScalarGridSpec(
            num_scalar_prefetch=1,
            grid=(nseq,),
            in_specs=[row_spec, row_spec, pl.BlockSpec(memory_space=pl.ANY)],
            out_specs=[pl.BlockSpec((1, nblk, N_HEADS, MOBA_BLOCK), lambda n, pt: (n, 0, 0, 0)),
                       pl.BlockSpec((1, k_eff, N_HEADS, LANES), lambda n, pt: (n, 0, 0, 0)),
                       pl.BlockSpec((1, N_HEADS, LANES), lambda n, pt: (n, 0, 0))],
            scratch_shapes=[pltpu.VMEM((ring, N_HEADS, HEAD_DIM, PAGE_SIZE), F32),
                            pltpu.VMEM((nblk, N_HEADS, MOBA_BLOCK), F32),
                            pltpu.SemaphoreType.DMA((ring,))]),
        out_shape=[jax.ShapeDtypeStruct((nseq, nblk, N_HEADS, MOBA_BLOCK), F32),
                   jax.ShapeDtypeStruct((nseq, k_eff, N_HEADS, LANES), jnp.int32),
                   jax.ShapeDtypeStruct((nseq, N_HEADS, LANES), F32)],
        compiler_params=_cparams(("arbitrary",)),
        name="sample_scores",
    )(page_table, q3, kn3, cache_kt)


def _sample_pv_kernel(pt_ref, idx_ref, p_ref, own_ref, vn_ref, cv_ref, out_ref,
                      vbuf_ref, sem_ref, *, k_eff):
    n = pl.program_id(0)
    nseq = pl.num_programs(0)
    ppb = MOBA_BLOCK // PAGE_SIZE

    def copies(seq, slot):
        out = []
        for h in range(N_HEADS):
            for k in range(k_eff):
                blk = idx_ref[seq, k, h]
                for j in range(ppb):
                    page = pt_ref[seq, blk * ppb + j]
                    out.append(pltpu.make_async_copy(
                        cv_ref.at[page, h], vbuf_ref.at[slot, h, k * ppb + j], sem_ref.at[slot]))
        return out

    slot = n % 2

    @pl.when(n == 0)
    def _():
        for c in copies(0, 0):
            c.start()

    @pl.when(n + 1 < nseq)
    def _():
        for c in copies(n + 1, 1 - slot):
            c.start()

    for c in copies(n, slot):
        c.wait()

    accs = []
    for h in range(N_HEADS):
        acc = jnp.zeros((HEAD_DIM, PAGE_SIZE), F32)
        for k in range(k_eff):
            pk = p_ref[0, idx_ref[n, k, h], h:h + 1, :]
            for j in range(ppb):
                acc = acc + pk[:, j * PAGE_SIZE:(j + 1) * PAGE_SIZE] * vbuf_ref[slot, h, k * ppb + j]
        accs.append(acc)
    own = own_ref[0]
    accs += [jnp.broadcast_to(own[h:h + 1], (HEAD_DIM, LANES)) for h in range(N_HEADS)]
    r = _lane_sums_as_rows(jnp.concatenate(accs, axis=0))
    out_ref[0] = r[0:1, :ATTN_WIDTH] + r[0:1, ATTN_WIDTH:] * (1.0 / LANES) * vn_ref[0]


def _sample_pv(page_table, idx, p, own, vn3, cache_vt, k_eff):
    nseq = vn3.shape[0]
    nblk = p.shape[1]
    ppb = MOBA_BLOCK // PAGE_SIZE
    row_spec = pl.BlockSpec((1, 1, ATTN_WIDTH), lambda n, pt, ix: (n, 0, 0))
    return pl.pallas_call(
        functools.partial(_sample_pv_kernel, k_eff=k_eff),
        grid_spec=pltpu.PrefetchScalarGridSpec(
            num_scalar_prefetch=2,
            grid=(nseq,),
            in_specs=[pl.BlockSpec((1, nblk, N_HEADS, MOBA_BLOCK), lambda n, pt, ix: (n, 0, 0, 0)),
                      pl.BlockSpec((1, N_HEADS, LANES), lambda n, pt, ix: (n, 0, 0)),
                      row_spec, pl.BlockSpec(memory_space=pl.ANY)],
            out_specs=row_spec,
            scratch_shapes=[pltpu.VMEM((2, N_HEADS, k_eff * ppb, HEAD_DIM, PAGE_SIZE), F32),
                            pltpu.SemaphoreType.DMA((2,))]),
        out_shape=jax.ShapeDtypeStruct((nseq, 1, ATTN_WIDTH), F32),
        compiler_params=_cparams(("arbitrary",)),
        name="sample_pv",
    )(page_table, idx, p, own, vn3, cache_vt)


def kernel(x_prompt, x_sample, cache_k, cache_v, state_pool, page_table, w_in, w_pool_group,
           pool_scale, w_pool_proj, w_attn_proj, w_out, ln1_g, ln1_b, w_router_group,
           b_router_group, w_router_expert, b_router_expert, w_exp_gate, w_exp_up,
           w_exp_down, ln2_g, ln2_b):
    assert w_in.shape[0] == DEPTH == 1
    n, s, d = x_prompt.shape
    nd, ld, _ = x_sample.shape
    assert ld == 1
    past_len = page_table.shape[1] * PAGE_SIZE
    assert past_len % MOBA_BLOCK == 0 and past_len // MOBA_BLOCK >= MOBA_TOPK
    layer = 0

    a0 = POOL_WIDTH
    win = w_in[layer]
    wn_f = jnp.concatenate([win[:, :a0], win[:, a0 + 2 * ATTN_WIDTH:]], axis=1)
    wt_f = win[:, a0:a0 + 2 * ATTN_WIDTH].T
    wn, wt = wn_f.astype(BF16), wt_f.astype(BF16)
    half = ROT_DIM // 2
    inv = (ROPE_THETA ** (-2.0 * jnp.arange(half, dtype=F32) / ROT_DIM)).reshape(half, 1)
    row = lambda a: a.reshape(1, -1)
    ng, ne = N_GROUPS, EXPERTS_PER_GROUP
    wr = jnp.zeros((d, ROUTER_LANES), F32)
    wr = wr.at[:, :ng].set(w_router_group[layer])
    wr = wr.at[:, EXPERT_ROW0:EXPERT_ROW0 + ng * ne].set(
        jnp.transpose(w_router_expert[layer], (1, 0, 2)).reshape(d, ng * ne))
    br = jnp.zeros((1, ROUTER_LANES), F32)
    br = br.at[0, :ng].set(b_router_group[layer])
    br = br.at[0, EXPERT_ROW0:EXPERT_ROW0 + ng * ne].set(b_router_expert[layer].reshape(-1))
    sub_f = (w_pool_group[layer], row(pool_scale[layer]), w_pool_proj[layer], w_attn_proj[layer],
             w_out[layer], row(ln1_g[layer]), row(ln1_b[layer]), wr, br)
    cast = (0, 2, 3, 4)
    sub_b = tuple(a.astype(BF16) if i in cast else a for i, a in enumerate(sub_f))
    weg = w_exp_gate[layer].astype(BF16)
    weu = w_exp_up[layer].astype(BF16)
    f = weg.shape[-1]
    wed = w_exp_down[layer].astype(BF16).reshape(ng, ne * f, d)
    g2, b2 = row(ln2_g[layer]), row(ln2_b[layer])

    tm = 512
    pooled, qt, k_p, kb, kmean, v_p, vt, gp, ga, utail = _proj_prompt(x_prompt, wn, wt, inv, tm)
    attn = _moba_prompt(qt, kb, vt, kmean)
    t = n * s
    flat = lambda a: a.reshape(t, a.shape[-1])
    h_p, w4_p, gsel_p = _sublayer1(flat(x_prompt), flat(pooled), flat(attn), flat(gp), flat(ga),
                                   sub_b, tm, False)
    y_p = _moe_grouped(h_p, gsel_p, w4_p, weg, weu, wed, g2, b2, 2 * tm).reshape(n, s, d)

    xs = x_sample.reshape(nd, d)
    state_t = jnp.transpose(state_pool[layer], (1, 0, 2))
    pooled_s, u_s, q_s, k_s, v_s, gp_s, ga_s = _proj_sample(xs, wn_f, wt_f, inv, state_t, past_len)
    cache_kt = jnp.transpose(cache_k[layer], (0, 2, 3, 1))
    cache_vt = jnp.transpose(cache_v[layer], (0, 2, 3, 1))
    r3 = lambda a: a.reshape(nd, 1, a.shape[-1])
    k_eff = MOBA_TOPK
    p_sel, idx, own = _sample_scores(page_table, r3(q_s), r3(k_s), cache_kt, k_eff)
    attn_s = _sample_pv(page_table, idx[:, :, :, 0], p_sel, own, r3(v_s), cache_vt,
                        k_eff).reshape(nd, ATTN_WIDTH)
    h_s, w4_s, _ = _sublayer1(xs, pooled_s, attn_s, gp_s, ga_s, sub_f, nd, True)
    y_s = _moe(h_s, w4_s, weg, weu, wed, g2, b2, nd).reshape(nd, 1, d)

    heads = lambda a, lead: a.reshape(lead + (N_HEADS, HEAD_DIM))
    pool_prompt = utail[:, HALO_ROWS - POOL_BUF:][None]
    pool_sample = jnp.concatenate([state_pool[layer][:, 1:], u_s[:, None, :]], axis=1)[None]
    rows = lambda a: jnp.transpose(a.reshape(n, N_HEADS, HEAD_DIM, s), (0, 3, 1, 2))[None]
    return (y_p, y_s.reshape(nd, ld, d),
            rows(k_p), rows(v_p), pool_prompt,
            heads(k_s, (1, nd, 1)), heads(v_s, (1, nd, 1)), pool_sample)
```

```python
import functools

import jax
import jax.numpy as jnp
from jax import lax
from jax.experimental import pallas as pl
from jax.experimental.pallas import tpu as pltpu

F32 = jnp.float32
BF16 = jnp.bfloat16

POOL_WINDOWS = (2, 4, 8, 16)
POOL_GROUP_WIDTH = 128
POOL_WIDTH = 512
POOL_BUF = 15
HEAD_DIM = 64
N_HEADS = 8
ATTN_WIDTH = 512
ROT_DIM = 16
ROPE_THETA = 500000.0
MOBA_BLOCK = 256
MOBA_TOPK = 3
PAGE_SIZE = 128
N_GROUPS = 4
EXPERTS_PER_GROUP = 8
EXPERT_TOPK = 2
DEPTH = 1
ALPHA = (2 * DEPTH) ** 0.25
LN_EPS = 1e-5
NEG_INF = -1e30

LANES = 128
SUBLANES = 8
HALO_ROWS = 16
VMEM_LIMIT = 56 * 1024 * 1024

ROUTER_LANES = 128
EXPERT_ROW0 = 8
PAIR = 2 * HEAD_DIM
PAGE_RING = 32
SCORE_LOOKAHEAD = 8
V_AUG = HEAD_DIM + 16
LOG2_E = 1.4426950408889634


def _cparams(sem):
    return pltpu.CompilerParams(dimension_semantics=sem, vmem_limit_bytes=VMEM_LIMIT)


def _dot(a, b):
    return jnp.dot(a, b, preferred_element_type=F32)


def _dot_nt(a, b):
    return lax.dot_general(a, b, (((1,), (1,)), ((), ())), preferred_element_type=F32)


def _split_bf16(x):
    hi = x.astype(BF16)
    lo = (x - hi.astype(F32)).astype(BF16)
    return hi, lo


def _dot3(a, b, dot=_dot):
    ah, al = _split_bf16(a)
    bh, bl = _split_bf16(b)
    return dot(ah, bh) + dot(ah, bl) + dot(al, bh)


def _dot_bf16(a, b):
    return _dot(a.astype(BF16), b.astype(BF16))


def _lane_sums_as_rows(x):
    ones = jnp.ones((SUBLANES, LANES), BF16)
    hi, lo = _split_bf16(x)
    return _dot_nt(ones, hi) + _dot_nt(ones, lo)


def _sigmoid(x):
    return 1.0 / (1.0 + jnp.exp(-x))


def _layer_norm(x, g, b):
    mu = jnp.mean(x, axis=-1, keepdims=True)
    xc = x - mu
    var = jnp.mean(xc * xc, axis=-1, keepdims=True)
    return xc * lax.rsqrt(var + LN_EPS) * g + b


def _rotary_t(zt, cos, sin):
    half = ROT_DIM // 2
    parts = []
    for h in range(N_HEADS):
        r0 = h * HEAD_DIM
        x1 = zt[r0:r0 + half]
        x2 = zt[r0 + half:r0 + ROT_DIM]
        parts += [x1 * cos - x2 * sin, x2 * cos + x1 * sin, zt[r0 + ROT_DIM:r0 + HEAD_DIM]]
    return jnp.concatenate(parts, axis=0)


def _cos_sin(inv, pos_row):
    ang = inv * pos_row
    return jnp.cos(ang), jnp.sin(ang)


def _first_argmax(work, idxf, axis, n):
    mx = jnp.max(work, axis=axis, keepdims=True)
    first = jnp.min(jnp.where(work == mx, idxf, float(n)), axis=axis, keepdims=True)
    return mx, first


def _proj_prompt_kernel(x_ref, wn_ref, wt_ref, inv_ref,
                        pooled_ref, qt_ref, k_ref, kb_ref, kmean_ref, v_ref, vt_ref,
                        gp_ref, ga_ref, utail_ref, halo_ref, *, tm):
    i = pl.program_id(1)
    nblk = tm // MOBA_BLOCK

    @pl.when(i == 0)
    def _():
        halo_ref[...] = jnp.zeros_like(halo_ref)

    xb = x_ref[0].astype(BF16)
    zn = _dot(xb, wn_ref[...])
    zt = _dot_nt(wt_ref[...], xb)

    u = zn[:, :POOL_WIDTH]
    ext = jnp.concatenate([halo_ref[...], u], axis=0)
    pos1 = i * tm + lax.broadcasted_iota(jnp.int32, (tm, POOL_GROUP_WIDTH), 0) + 1
    outs = []
    for g, w in enumerate(POOL_WINDOWS):
        c0 = g * POOL_GROUP_WIDTH
        s = ext[:, c0:c0 + POOL_GROUP_WIDTH]
        k = 1
        while k < w:
            n = s.shape[0]
            s = s[k:] + s[:n - k]
            k *= 2
        s = s[HALO_ROWS + 1 - w:HALO_ROWS + 1 - w + tm]
        cnt = jnp.minimum(pos1, w).astype(F32)
        outs.append(s / cnt - u[:, c0:c0 + POOL_GROUP_WIDTH])
    pooled_ref[0] = jnp.concatenate(outs, axis=1).astype(BF16)
    halo_ref[...] = u[tm - HALO_ROWS:]
    utail_ref[0] = u[tm - HALO_ROWS:]

    vt_f = zn[:, POOL_WIDTH:POOL_WIDTH + ATTN_WIDTH].T
    v_ref[0] = vt_f
    vt = vt_f.astype(BF16)
    gates = _sigmoid(zn[:, POOL_WIDTH + ATTN_WIDTH:])
    d = gp_ref.shape[-1]
    gp_ref[0] = gates[:, :d]
    ga_ref[0] = gates[:, d:]

    pos = (i * tm + lax.broadcasted_iota(jnp.int32, (1, tm), 1)).astype(F32)
    cos, sin = _cos_sin(inv_ref[...], pos)
    qt = _rotary_t(zt[:ATTN_WIDTH], cos, sin) * (HEAD_DIM ** -0.5 * LOG2_E)
    kt_f = _rotary_t(zt[ATTN_WIDTH:], cos, sin)
    k_ref[0] = kt_f
    kn = kt_f.T
    kb_ref[0] = kn.astype(BF16)
    for j in range(nblk):
        sl = slice(j * MOBA_BLOCK, (j + 1) * MOBA_BLOCK)
        qt_ref[0, j] = qt[:, sl]
        ones = jnp.ones((V_AUG - HEAD_DIM, MOBA_BLOCK), BF16)
        vt_ref[0, j] = jnp.concatenate(
            [piece for h in range(N_HEADS)
             for piece in (vt[h * HEAD_DIM:(h + 1) * HEAD_DIM, sl], ones)], axis=0)
        kmean_ref[0, pl.ds(i * nblk + j, 1), :] = (
            jnp.sum(kn[sl], axis=0, keepdims=True) * (1.0 / MOBA_BLOCK))


def _proj_prompt(x, wn, wt, inv, tm):
    n, s, d = x.shape
    nb = s // MOBA_BLOCK
    nblk = tm // MOBA_BLOCK
    tok = lambda c, dt: jax.ShapeDtypeStruct((n, s, c), dt)
    blk = lambda dt: jax.ShapeDtypeStruct((n, nb, ATTN_WIDTH, MOBA_BLOCK), dt)
    tok_spec = lambda c: pl.BlockSpec((1, tm, c), lambda b, i: (b, i, 0))
    blk_spec = pl.BlockSpec((1, nblk, ATTN_WIDTH, MOBA_BLOCK), lambda b, i: (b, i, 0, 0))
    full = lambda a: pl.BlockSpec(a.shape, lambda b, i: (0,) * a.ndim)
    tr = jax.ShapeDtypeStruct((n, ATTN_WIDTH, s), F32)
    tr_spec = pl.BlockSpec((1, ATTN_WIDTH, tm), lambda b, i: (b, 0, i))
    return pl.pallas_call(
        functools.partial(_proj_prompt_kernel, tm=tm),
        grid=(n, s // tm),
        in_specs=[tok_spec(d), full(wn), full(wt), full(inv)],
        out_specs=[tok_spec(POOL_WIDTH), blk_spec, tr_spec, tok_spec(ATTN_WIDTH),
                   pl.BlockSpec((1, nb, ATTN_WIDTH), lambda b, i: (b, 0, 0)),
                   tr_spec,
                   pl.BlockSpec((1, nblk, N_HEADS * V_AUG, MOBA_BLOCK), lambda b, i: (b, i, 0, 0)),
                   tok_spec(d), tok_spec(d),
                   pl.BlockSpec((1, HALO_ROWS, POOL_WIDTH), lambda b, i: (b, 0, 0))],
        out_shape=[tok(POOL_WIDTH, BF16), blk(F32), tr, tok(ATTN_WIDTH, BF16),
                   jax.ShapeDtypeStruct((n, nb, ATTN_WIDTH), F32),
                   tr, jax.ShapeDtypeStruct((n, nb, N_HEADS * V_AUG, MOBA_BLOCK), BF16),
                   tok(d, F32), tok(d, F32),
                   jax.ShapeDtypeStruct((n, HALO_ROWS, POOL_WIDTH), F32)],
        scratch_shapes=[pltpu.VMEM((HALO_ROWS, POOL_WIDTH), F32)],
        compiler_params=_cparams(("arbitrary", "arbitrary")),
        name="proj_prompt",
    )(x, wn, wt, inv)


def _proj_sample_kernel(x_ref, wn_ref, wt_ref, inv_ref, st_ref,
                        pooled_ref, u_ref, q_ref, k_ref, v_ref, gp_ref, ga_ref, *, past_len):
    x = x_ref[...]
    uv_w = POOL_WIDTH + ATTN_WIDTH
    zn = _dot3(x, wn_ref[:, :uv_w])
    zt = _dot3(wt_ref[...], x, _dot_nt)
    m = x.shape[0]

    u = zn[:, :POOL_WIDTH]
    u_ref[...] = u
    outs = []
    for g, w in enumerate(POOL_WINDOWS):
        c0 = g * POOL_GROUP_WIDTH
        s = u[:, c0:c0 + POOL_GROUP_WIDTH]
        for r in range(POOL_BUF - (w - 1), POOL_BUF):
            s = s + st_ref[r, :, c0:c0 + POOL_GROUP_WIDTH]
        cnt = float(min(past_len + 1, w))
        outs.append(s / cnt - u[:, c0:c0 + POOL_GROUP_WIDTH])
    pooled_ref[...] = jnp.concatenate(outs, axis=1)

    v_ref[...] = zn[:, POOL_WIDTH:]
    gates = _sigmoid(_dot_bf16(x, wn_ref[:, uv_w:]))
    d = gp_ref.shape[-1]
    gp_ref[...] = gates[:, :d]
    ga_ref[...] = gates[:, d:]

    pos = jnp.full((1, m), float(past_len), F32)
    cos, sin = _cos_sin(inv_ref[...], pos)
    q_ref[...] = (_rotary_t(zt[:ATTN_WIDTH], cos, sin) * (HEAD_DIM ** -0.5)).T
    k_ref[...] = _rotary_t(zt[ATTN_WIDTH:], cos, sin).T


def _proj_sample(x, wn, wt, inv, state_t, past_len):
    m, d = x.shape
    o = lambda c: jax.ShapeDtypeStruct((m, c), F32)
    return pl.pallas_call(
        functools.partial(_proj_sample_kernel, past_len=past_len),
        out_shape=[o(POOL_WIDTH), o(POOL_WIDTH), o(ATTN_WIDTH), o(ATTN_WIDTH),
                   o(ATTN_WIDTH), o(d), o(d)],
        compiler_params=pltpu.CompilerParams(vmem_limit_bytes=VMEM_LIMIT),
        name="proj_sample",
    )(x, wn, wt, inv, state_t)


def _moba_prompt_kernel(qt_ref, kb_ref, vt_ref, kmean_ref, out_ref,
                        sel_ref, qbd_ref, m_ref, acc_ref):
    blk = pl.program_id(1)
    nb = kmean_ref.shape[1]
    npair = ATTN_WIDTH // PAIR
    two = 2 * MOBA_BLOCK
    rows = lax.broadcasted_iota(jnp.int32, (PAIR, MOBA_BLOCK), 0)
    jj = lax.broadcasted_iota(jnp.int32, (nb, two), 0)
    jf = jj.astype(F32)
    valid = jj < blk

    for p in range(npair):
        ps = slice(p * PAIR, (p + 1) * PAIR)
        qt = qt_ref[0, 0, ps, :]
        qbd = jnp.concatenate([jnp.where(rows < HEAD_DIM, qt, 0.0),
                               jnp.where(rows >= HEAD_DIM, qt, 0.0)], axis=1)
        qbd_ref[p] = qbd.astype(BF16)
        work = jnp.where(valid, _dot3(kmean_ref[0, :, ps], qbd), NEG_INF)
        sel = jnp.zeros((nb, two), jnp.bool_)
        for _ in range(min(MOBA_TOPK, nb - 1)):
            _, first = _first_argmax(work, jf, 0, nb)
            pick = jf == first
            sel = jnp.logical_or(sel, pick)
            work = jnp.where(pick, -jnp.inf, work)
        sel_ref[p] = jnp.where(jnp.logical_and(sel, valid), 1.0, 0.0)
    for h in range(N_HEADS):
        m_ref[h] = jnp.full((1, MOBA_BLOCK), NEG_INF, F32)
        acc_ref[h] = jnp.zeros((V_AUG, MOBA_BLOCK), F32)

    half = MOBA_BLOCK // 2
    kpos = lax.broadcasted_iota(jnp.int32, (half, MOBA_BLOCK), 0)
    qpos = lax.broadcasted_iota(jnp.int32, (half, MOBA_BLOCK), 1)

    def attend(j, past):
        row0 = pl.multiple_of(j * MOBA_BLOCK, MOBA_BLOCK)

        def scores(h):
            p, a = divmod(h, 2)
            out = []
            for kh in range(2):
                kt = kb_ref[0, pl.ds(row0 + kh * half, half), p * PAIR:(p + 1) * PAIR]
                st = _dot(kt, qbd_ref[p, :, a * MOBA_BLOCK:(a + 1) * MOBA_BLOCK])
                if not past:
                    st = jnp.where(kpos + kh * half <= qpos, st, NEG_INF)
                out.append(st)
            return out

        sts = {h: scores(h) for h in range(min(SCORE_LOOKAHEAD, N_HEADS))}
        for h in range(N_HEADS):
            if h + SCORE_LOOKAHEAD < N_HEADS:
                sts[h + SCORE_LOOKAHEAD] = scores(h + SCORE_LOOKAHEAD)
            p, a = divmod(h, 2)
            qs = slice(a * MOBA_BLOCK, (a + 1) * MOBA_BLOCK)
            st0, st1 = sts.pop(h)
            m = m_ref[h]
            mj = jnp.maximum(jnp.max(st0, axis=0, keepdims=True), jnp.max(st1, axis=0, keepdims=True))
            if past:
                on = sel_ref[p, pl.ds(j, 1), qs]
                m_new = jnp.where(on > 0.0, jnp.maximum(m, mj), m)
                shift = jnp.where(on > 0.0, m_new, mj)
            else:
                m_new = jnp.maximum(m, mj)
                shift = m_new
            pr0 = jnp.exp2(st0 - shift)
            pr1 = jnp.exp2(st1 - shift)
            alpha = jnp.exp2(m - m_new)
            hs = slice(h * V_AUG, (h + 1) * V_AUG)
            o = (_dot(vt_ref[0, j, hs, :half], pr0.astype(BF16))
                 + _dot(vt_ref[0, j, hs, half:], pr1.astype(BF16)))
            if past:
                o = on * o
            m_ref[h] = m_new
            acc_ref[h] = alpha * acc_ref[h] + o

    def past_block(j, carry):
        attend(j, True)
        return carry

    lax.fori_loop(0, blk, past_block, 0)
    attend(blk, False)

    def normalized(h):
        acc = acc_ref[h]
        return acc[:HEAD_DIM] / acc[HEAD_DIM:HEAD_DIM + 1]

    for p in range(npair):
        o = jnp.concatenate([normalized(2 * p), normalized(2 * p + 1)], axis=0)
        out_ref[0, :, p * PAIR:(p + 1) * PAIR] = o.T


def _moba_prompt(qt, kb, vt, kmean):
    n, nb, _, _ = qt.shape
    s = kb.shape[1]
    npair = ATTN_WIDTH // PAIR
    return pl.pallas_call(
        _moba_prompt_kernel,
        grid=(n, nb),
        in_specs=[pl.BlockSpec((1, 1, ATTN_WIDTH, MOBA_BLOCK), lambda b, j: (b, j, 0, 0)),
                  pl.BlockSpec((1, s, ATTN_WIDTH), lambda b, j: (b, 0, 0)),
                  pl.BlockSpec((1, nb, N_HEADS * V_AUG, MOBA_BLOCK), lambda b, j: (b, 0, 0, 0)),
                  pl.BlockSpec((1, nb, ATTN_WIDTH), lambda b, j: (b, 0, 0))],
        out_specs=pl.BlockSpec((1, MOBA_BLOCK, ATTN_WIDTH), lambda b, j: (b, j, 0)),
        out_shape=jax.ShapeDtypeStruct((n, s, ATTN_WIDTH), F32),
        scratch_shapes=[pltpu.VMEM((npair, nb, 2 * MOBA_BLOCK), F32),
                        pltpu.VMEM((npair, PAIR, 2 * MOBA_BLOCK), BF16),
                        pltpu.VMEM((N_HEADS, 1, MOBA_BLOCK), F32),
                        pltpu.VMEM((N_HEADS, V_AUG, MOBA_BLOCK), F32)],
        compiler_params=_cparams(("arbitrary", "arbitrary")),
        name="moba_prompt",
    )(qt, kb, vt, kmean)


def _sublayer1_kernel(x_ref, pooled_ref, attn_ref, gp_ref, ga_ref, wg_ref, scale_ref, wpp_ref, wap_ref,
                      wout_ref, g1_ref, b1_ref, wr_ref, br_ref, h_ref, w4_ref, gsel_ref, *, precise):
    tm = x_ref.shape[0]
    mm = _dot3 if precise else _dot_bf16
    pooled = pooled_ref[...]
    py = jnp.concatenate(
        [mm(pooled[:, g * POOL_GROUP_WIDTH:(g + 1) * POOL_GROUP_WIDTH], wg_ref[g])
         for g in range(len(POOL_WINDOWS))], axis=1) * scale_ref[...]
    a = mm(py, wpp_ref[...])
    b = mm(attn_ref[...], wap_ref[...])
    merged = gp_ref[...] * a + ga_ref[...] * b
    h = _layer_norm(ALPHA * x_ref[...] + mm(merged, wout_ref[...]), g1_ref[...], b1_ref[...])
    h_ref[...] = h

    lt = (_dot3(h, wr_ref[...]) + br_ref[...]).T
    idx = lax.broadcasted_iota(jnp.int32, (SUBLANES, tm), 0)
    idf = idx.astype(F32)
    gl = jnp.where(idx < N_GROUPS, lt[:SUBLANES], -jnp.inf)
    gmax, gsel = _first_argmax(gl, idf, 0, SUBLANES)
    gsel_ref[0] = gsel
    gexp = jnp.exp(gl - gmax)
    gprob = gexp / jnp.sum(gexp, axis=0, keepdims=True)
    pg = jnp.sum(jnp.where(idf == gsel, gprob, 0.0), axis=0, keepdims=True)
    elog = jnp.zeros((SUBLANES, tm), F32)
    for g in range(N_GROUPS):
        r0 = EXPERT_ROW0 + g * EXPERTS_PER_GROUP
        elog = jnp.where(gsel == float(g), lt[r0:r0 + EXPERTS_PER_GROUP], elog)
    e1, i1 = _first_argmax(elog, idf, 0, SUBLANES)
    e2, i2 = _first_argmax(jnp.where(idf == i1, -jnp.inf, elog), idf, 0, SUBLANES)
    x2 = jnp.exp(e2 - e1)
    den = 1.0 + x2
    w_e = jnp.where(idf == i1, 1.0 / den, 0.0) + jnp.where(idf == i2, x2 / den, 0.0)
    pw = pg * w_e
    slabs = [jnp.where(gsel == float(g), pw, 0.0) for g in range(N_GROUPS)]
    slabs.append(jnp.zeros((ROUTER_LANES - N_GROUPS * EXPERTS_PER_GROUP, tm), F32))
    wn = jnp.concatenate(slabs, axis=0).T
    for g in range(N_GROUPS):
        sh = (ROUTER_LANES - g * EXPERTS_PER_GROUP) % ROUTER_LANES
        w4_ref[g] = wn if sh == 0 else pltpu.roll(wn, sh, 1)


def _sublayer1(x, pooled, attn, gp, ga, wts, tm, precise):
    t, d = x.shape
    row = lambda c: pl.BlockSpec((tm, c), lambda i: (i, 0))
    full = lambda a: pl.BlockSpec(a.shape, lambda i: (0,) * a.ndim)
    return pl.pallas_call(
        functools.partial(_sublayer1_kernel, precise=precise),
        grid=(t // tm,),
        in_specs=[row(d), row(POOL_WIDTH), row(ATTN_WIDTH), row(d), row(d)] + [full(a) for a in wts],
        out_specs=[row(d), pl.BlockSpec((N_GROUPS, tm, ROUTER_LANES), lambda i: (0, i, 0)),
                   pl.BlockSpec((1, 1, tm), lambda i: (i, 0, 0))],
        out_shape=[jax.ShapeDtypeStruct((t, d), F32),
                   jax.ShapeDtypeStruct((N_GROUPS, t, ROUTER_LANES), F32),
                   jax.ShapeDtypeStruct((t // tm, 1, tm), F32)],
        compiler_params=_cparams(("arbitrary",)),
        name="sublayer1",
    )(x, pooled, attn, gp, ga, *wts)


def _moe_kernel(h_ref, w_ref, weg_ref, weu_ref, wed_ref, g2_ref, b2_ref, y_ref, acc_ref):
    g = pl.program_id(1)

    @pl.when(g == 0)
    def _():
        acc_ref[...] = jnp.zeros_like(acc_ref)

    h = h_ref[...]
    hb = h.astype(BF16)
    w = w_ref[0]
    hid = []
    for e in range(EXPERTS_PER_GROUP):
        a = _dot(hb, weg_ref[0, e])
        b = _dot(hb, weu_ref[0, e])
        hid.append((a * _sigmoid(a) * b * w[:, e:e + 1]).astype(BF16))
    acc_ref[...] += _dot(jnp.concatenate(hid, axis=1), wed_ref[0])

    @pl.when(g == pl.num_programs(1) - 1)
    def _():
        y_ref[...] = _layer_norm(ALPHA * h + acc_ref[...], g2_ref[...], b2_ref[...])


def _moe(h, w4, weg, weu, wed, g2, b2, tm):
    t, d = h.shape
    ng, ne, _, f = weg.shape
    return pl.pallas_call(
        _moe_kernel,
        grid=(t // tm, ng),
        in_specs=[pl.BlockSpec((tm, d), lambda i, g: (i, 0)),
                  pl.BlockSpec((1, tm, ROUTER_LANES), lambda i, g: (g, i, 0)),
                  pl.BlockSpec((1, ne, d, f), lambda i, g: (g, 0, 0, 0)),
                  pl.BlockSpec((1, ne, d, f), lambda i, g: (g, 0, 0, 0)),
                  pl.BlockSpec((1, ne * f, d), lambda i, g: (g, 0, 0)),
                  pl.BlockSpec(g2.shape, lambda i, g: (0, 0)),
                  pl.BlockSpec(b2.shape, lambda i, g: (0, 0))],
        out_specs=pl.BlockSpec((tm, d), lambda i, g: (i, 0)),
        out_shape=jax.ShapeDtypeStruct((t, d), F32),
        scratch_shapes=[pltpu.VMEM((tm, d), F32)],
        compiler_params=_cparams(("arbitrary", "arbitrary")),
        name="moe",
    )(h, w4, weg, weu, wed, g2, b2)


MOE_CAP = 320


def _lane_prefix_sum(x):
    n = x.shape[1]
    lane = lax.broadcasted_iota(jnp.int32, x.shape, 1)
    s = 1
    while s < n:
        x = x + jnp.where(lane >= s, pltpu.roll(x, s, 1), 0.0)
        s *= 2
    return x


def _moe_grouped_kernel(h_ref, gsel_ref, w_ref, weg_ref, weu_ref, wed_ref, g2_ref, b2_ref, y_ref,
                        hb_ref, acc_ref, rank_ref, rankt_ref):
    g = pl.program_id(1)
    tm = h_ref.shape[0]
    cap = MOE_CAP

    @pl.when(g == 0)
    def _():
        hb_ref[...] = h_ref[...].astype(BF16)
        acc_ref[...] = jnp.zeros_like(acc_ref)
        sub = lax.broadcasted_iota(jnp.int32, (SUBLANES, tm), 0).astype(F32)
        member = jnp.where(sub == gsel_ref[0], 1.0, 0.0)
        rank = jnp.where(member > 0.0, _lane_prefix_sum(member) - 1.0, -1.0)
        rank_ref[...] = rank
        rank_t = jnp.concatenate([rank, jnp.full((LANES - SUBLANES, tm), -1.0, F32)], axis=0).T
        for gg in range(N_GROUPS):
            rankt_ref[gg] = jnp.broadcast_to(rank_t[:, gg:gg + 1], (tm, LANES))

    rrow = rank_ref[pl.ds(g, 1), :]
    rcol = rankt_ref[g]
    wh, wl = _split_bf16(w_ref[0])
    count = jnp.max(rrow).astype(jnp.int32) + 1
    n_pass = (count + (cap - 1)) // cap
    sub_i = lax.broadcasted_iota(jnp.int32, (cap, tm), 0).astype(F32)
    lane_i = lax.broadcasted_iota(jnp.int32, (tm, LANES), 1).astype(F32)

    def one_pass(ps, carry):
        base = (ps * cap).astype(F32)
        pick = jnp.where(rrow - base == sub_i, 1.0, 0.0).astype(BF16)
        xg = _dot(pick, hb_ref[...]).astype(BF16)
        wg = _dot(pick, wh) + _dot(pick, wl)
        hid = []
        for e in range(EXPERTS_PER_GROUP):
            a = _dot(xg, weg_ref[0, e])
            b = _dot(xg, weu_ref[0, e])
            hid.append((a * _sigmoid(a) * b * wg[:, e:e + 1]).astype(BF16))
        yg = _dot(jnp.concatenate(hid, axis=1), wed_ref[0]).astype(BF16)
        cap_pad = -(-cap // LANES) * LANES
        if cap_pad > cap:
            yg = jnp.concatenate([yg, jnp.zeros((cap_pad - cap, yg.shape[1]), BF16)], axis=0)
        put = jnp.concatenate(
            [jnp.where(rcol - base == lane_i + float(k * LANES), 1.0, 0.0) for k in range(cap_pad // LANES)],
            axis=1).astype(BF16)
        acc_ref[...] += _dot(put, yg)
        return carry

    lax.fori_loop(0, n_pass, one_pass, 0)

    @pl.when(g == pl.num_programs(1) - 1)
    def _():
        y_ref[...] = _layer_norm(ALPHA * h_ref[...] + acc_ref[...], g2_ref[...], b2_ref[...])


def _moe_grouped(h, gsel, w4, weg, weu, wed, g2, b2, tm):
    t, d = h.shape
    ng, ne, _, f = weg.shape
    gsel = gsel.reshape(t // tm, 1, tm)
    return pl.pallas_call(
        _moe_grouped_kernel,
        grid=(t // tm, ng),
        in_specs=[pl.BlockSpec((tm, d), lambda i, g: (i, 0)),
                  pl.BlockSpec((1, 1, tm), lambda i, g: (i, 0, 0)),
                  pl.BlockSpec((1, tm, ROUTER_LANES), lambda i, g: (g, i, 0)),
                  pl.BlockSpec((1, ne, d, f), lambda i, g: (g, 0, 0, 0)),
                  pl.BlockSpec((1, ne, d, f), lambda i, g: (g, 0, 0, 0)),
                  pl.BlockSpec((1, ne * f, d), lambda i, g: (g, 0, 0)),
                  pl.BlockSpec(g2.shape, lambda i, g: (0, 0)),
                  pl.BlockSpec(b2.shape, lambda i, g: (0, 0))],
        out_specs=pl.BlockSpec((tm, d), lambda i, g: (i, 0)),
        out_shape=jax.ShapeDtypeStruct((t, d), F32),
        scratch_shapes=[pltpu.VMEM((tm, d), BF16), pltpu.VMEM((tm, d), F32),
                        pltpu.VMEM((SUBLANES, tm), F32), pltpu.VMEM((N_GROUPS, tm, LANES), F32)],
        compiler_params=_cparams(("arbitrary", "arbitrary")),
        name="moe_grouped",
    )(h, gsel, w4, weg, weu, wed, g2, b2)


def _lane_rep(row):
    return jnp.broadcast_to(row, (LANES, ATTN_WIDTH)).T.reshape(N_HEADS, HEAD_DIM, LANES)


def _sample_scores_kernel(pt_ref, q_ref, kn_ref, ck_ref, p_ref, idx_ref, own_ref,
                          buf_ref, lg_ref, sem_ref, *, n_pages, k_eff):
    n = pl.program_id(0)
    nseq = pl.num_programs(0)
    ppb = MOBA_BLOCK // PAGE_SIZE
    nblk = n_pages // ppb

    def page_copy(page, slot):
        return pltpu.make_async_copy(ck_ref.at[page], buf_ref.at[slot], sem_ref.at[slot])

    ring = buf_ref.shape[0]

    @pl.when(n == 0)
    def _():
        for s in range(ring):
            page_copy(pt_ref[0, s], s).start()

    qrep = _lane_rep(q_ref[0])
    for i in range(n_pages):
        slot = i % ring
        page_copy(pt_ref[n, i], slot).wait()
        lg = jnp.sum(buf_ref[slot] * qrep, axis=1)
        nxt = i + ring
        if nxt < n_pages:
            page_copy(pt_ref[n, nxt], slot).start()
        else:
            @pl.when(n + 1 < nseq)
            def _():
                page_copy(pt_ref[n + 1, nxt - n_pages], slot).start()
        c0 = (i % ppb) * PAGE_SIZE
        lg_ref[i // ppb, :, c0:c0 + PAGE_SIZE] = lg

    logits = lg_ref[...]
    work = jnp.sum(logits, axis=2, keepdims=True) * (1.0 / MOBA_BLOCK)
    bf = lax.broadcasted_iota(jnp.int32, work.shape, 0).astype(F32)
    sel = jnp.zeros(work.shape, jnp.bool_)
    for k in range(k_eff):
        _, first = _first_argmax(work, bf, 0, nblk)
        idx_ref[0, k] = jnp.broadcast_to(first[0], (N_HEADS, LANES)).astype(jnp.int32)
        pick = bf == first
        sel = jnp.logical_or(sel, pick)
        work = jnp.where(pick, -jnp.inf, work)

    s_own = jnp.sum(_lane_rep(kn_ref[0]) * qrep, axis=1)[None, :, :1]
    lm = jnp.where(sel, logits, NEG_INF)
    m = jnp.maximum(jnp.max(jnp.max(lm, axis=2, keepdims=True), axis=0, keepdims=True), s_own)
    p = jnp.exp(lm - m)
    p_own = jnp.exp(s_own - m)
    l = jnp.sum(jnp.sum(p, axis=2, keepdims=True), axis=0, keepdims=True) + p_own
    p_ref[0] = p / l
    own_ref[0] = jnp.broadcast_to((p_own / l)[0], (N_HEADS, LANES))


def _sample_scores(page_table, q3, kn3, cache_kt, k_eff):
    nseq, n_pages = page_table.shape
    nblk = n_pages * PAGE_SIZE // MOBA_BLOCK
    ring = min(PAGE_RING, n_pages)
    row_spec = pl.BlockSpec((1, 1, ATTN_WIDTH), lambda n, pt: (n, 0, 0))
    return pl.pallas_call(
        functools.partial(_sample_scores_kernel, n_pages=n_pages, k_eff=k_eff),
        grid_spec=pltpu.PrefetchScalarGridSpec(
            num_scalar_prefetch=1,
            grid=(nseq,),
            in_specs=[row_spec, row_spec, pl.BlockSpec(memory_space=pl.ANY)],
            out_specs=[pl.BlockSpec((1, nblk, N_HEADS, MOBA_BLOCK), lambda n, pt: (n, 0, 0, 0)),
                       pl.BlockSpec((1, k_eff, N_HEADS, LANES), lambda n, pt: (n, 0, 0, 0)),
                       pl.BlockSpec((1, N_HEADS, LANES), lambda n, pt: (n, 0, 0))],
            scratch_shapes=[pltpu.VMEM((ring, N_HEADS, HEAD_DIM, PAGE_SIZE), F32),
                            pltpu.VMEM((nblk, N_HEADS, MOBA_BLOCK), F32),
                            pltpu.SemaphoreType.DMA((ring,))]),
        out_shape=[jax.ShapeDtypeStruct((nseq, nblk, N_HEADS, MOBA_BLOCK), F32),
                   jax.ShapeDtypeStruct((nseq, k_eff, N_HEADS, LANES), jnp.int32),
                   jax.ShapeDtypeStruct((nseq, N_HEADS, LANES), F32)],
        compiler_params=_cparams(("arbitrary",)),
        name="sample_scores",
    )(page_table, q3, kn3, cache_kt)


def _sample_pv_kernel(pt_ref, idx_ref, p_ref, own_ref, vn_ref, cv_ref, out_ref,
                      vbuf_ref, sem_ref, *, k_eff):
    n = pl.program_id(0)
    nseq = pl.num_programs(0)
    ppb = MOBA_BLOCK // PAGE_SIZE

    def copies(seq, slot):
        out = []
        for h in range(N_HEADS):
            for k in range(k_eff):
                blk = idx_ref[seq, k, h]
                for j in range(ppb):
                    page = pt_ref[seq, blk * ppb + j]
                    out.append(pltpu.make_async_copy(
                        cv_ref.at[page, h], vbuf_ref.at[slot, h, k * ppb + j], sem_ref.at[slot]))
        return out

    slot = n % 2

    @pl.when(n == 0)
    def _():
        for c in copies(0, 0):
            c.start()

    @pl.when(n + 1 < nseq)
    def _():
        for c in copies(n + 1, 1 - slot):
            c.start()

    for c in copies(n, slot):
        c.wait()

    accs = []
    for h in range(N_HEADS):
        acc = jnp.zeros((HEAD_DIM, PAGE_SIZE), F32)
        for k in range(k_eff):
            pk = p_ref[0, idx_ref[n, k, h], h:h + 1, :]
            for j in range(ppb):
                acc = acc + pk[:, j * PAGE_SIZE:(j + 1) * PAGE_SIZE] * vbuf_ref[slot, h, k * ppb + j]
        accs.append(acc)
    own = own_ref[0]
    accs += [jnp.broadcast_to(own[h:h + 1], (HEAD_DIM, LANES)) for h in range(N_HEADS)]
    r = _lane_sums_as_rows(jnp.concatenate(accs, axis=0))
    out_ref[0] = r[0:1, :ATTN_WIDTH] + r[0:1, ATTN_WIDTH:] * (1.0 / LANES) * vn_ref[0]


def _sample_pv(page_table, idx, p, own, vn3, cache_vt, k_eff):
    nseq = vn3.shape[0]
    nblk = p.shape[1]
    ppb = MOBA_BLOCK // PAGE_SIZE
    row_spec = pl.BlockSpec((1, 1, ATTN_WIDTH), lambda n, pt, ix: (n, 0, 0))
    return pl.pallas_call(
        functools.partial(_sample_pv_kernel, k_eff=k_eff),
        grid_spec=pltpu.PrefetchScalarGridSpec(
            num_scalar_prefetch=2,
            grid=(nseq,),
            in_specs=[pl.BlockSpec((1, nblk, N_HEADS, MOBA_BLOCK), lambda n, pt, ix: (n, 0, 0, 0)),
                      pl.BlockSpec((1, N_HEADS, LANES), lambda n, pt, ix: (n, 0, 0)),
                      row_spec, pl.BlockSpec(memory_space=pl.ANY)],
            out_specs=row_spec,
            scratch_shapes=[pltpu.VMEM((2, N_HEADS, k_eff * ppb, HEAD_DIM, PAGE_SIZE), F32),
                            pltpu.SemaphoreType.DMA((2,))]),
        out_shape=jax.ShapeDtypeStruct((nseq, 1, ATTN_WIDTH), F32),
        compiler_params=_cparams(("arbitrary",)),
        name="sample_pv",
    )(page_table, idx, p, own, vn3, cache_vt)


def kernel(x_prompt, x_sample, cache_k, cache_v, state_pool, page_table, w_in, w_pool_group,
           pool_scale, w_pool_proj, w_attn_proj, w_out, ln1_g, ln1_b, w_router_group,
           b_router_group, w_router_expert, b_router_expert, w_exp_gate, w_exp_up,
           w_exp_down, ln2_g, ln2_b):
    assert w_in.shape[0] == DEPTH == 1
    n, s, d = x_prompt.shape
    nd, ld, _ = x_sample.shape
    assert ld == 1
    past_len = page_table.shape[1] * PAGE_SIZE
    assert past_len % MOBA_BLOCK == 0 and past_len // MOBA_BLOCK >= MOBA_TOPK
    layer = 0

    a0 = POOL_WIDTH
    win = w_in[layer]
    wn_f = jnp.concatenate([win[:, :a0], win[:, a0 + 2 * ATTN_WIDTH:]], axis=1)
    wt_f = win[:, a0:a0 + 2 * ATTN_WIDTH].T
    wn, wt = wn_f.astype(BF16), wt_f.astype(BF16)
    half = ROT_DIM // 2
    inv = (ROPE_THETA ** (-2.0 * jnp.arange(half, dtype=F32) / ROT_DIM)).reshape(half, 1)
    row = lambda a: a.reshape(1, -1)
    ng, ne = N_GROUPS, EXPERTS_PER_GROUP
    wr = jnp.zeros((d, ROUTER_LANES), F32)
    wr = wr.at[:, :ng].set(w_router_group[layer])
    wr = wr.at[:, EXPERT_ROW0:EXPERT_ROW0 + ng * ne].set(
        jnp.transpose(w_router_expert[layer], (1, 0, 2)).reshape(d, ng * ne))
    br = jnp.zeros((1, ROUTER_LANES), F32)
    br = br.at[0, :ng].set(b_router_group[layer])
    br = br.at[0, EXPERT_ROW0:EXPERT_ROW0 + ng * ne].set(b_router_expert[layer].reshape(-1))
    sub_f = (w_pool_group[layer], row(pool_scale[layer]), w_pool_proj[layer], w_attn_proj[layer],
             w_out[layer], row(ln1_g[layer]), row(ln1_b[layer]), wr, br)
    cast = (0, 2, 3, 4)
    sub_b = tuple(a.astype(BF16) if i in cast else a for i, a in enumerate(sub_f))
    weg = w_exp_gate[layer].astype(BF16)
    weu = w_exp_up[layer].astype(BF16)
    f = weg.shape[-1]
    wed = w_exp_down[layer].astype(BF16).reshape(ng, ne * f, d)
    g2, b2 = row(ln2_g[layer]), row(ln2_b[layer])

    tm = 512
    pooled, qt, k_p, kb, kmean, v_p, vt, gp, ga, utail = _proj_prompt(x_prompt, wn, wt, inv, tm)
    attn = _moba_prompt(qt, kb, vt, kmean)
    t = n * s
    flat = lambda a: a.reshape(t, a.shape[-1])
    h_p, w4_p, gsel_p = _sublayer1(flat(x_prompt), flat(pooled), flat(attn), flat(gp), flat(ga),
                                   sub_b, tm, False)
    y_p = _moe_grouped(h_p, gsel_p, w4_p, weg, weu, wed, g2, b2, 2 * tm).reshape(n, s, d)

    xs = x_sample.reshape(nd, d)
    state_t = jnp.transpose(state_pool[layer], (1, 0, 2))
    pooled_s, u_s, q_s, k_s, v_s, gp_s, ga_s = _proj_sample(xs, wn_f, wt_f, inv, state_t, past_len)
    cache_kt = jnp.transpose(cache_k[layer], (0, 2, 3, 1))
    cache_vt = jnp.transpose(cache_v[layer], (0, 2, 3, 1))
    r3 = lambda a: a.reshape(nd, 1, a.shape[-1])
    k_eff = MOBA_TOPK
    p_sel, idx, own = _sample_scores(page_table, r3(q_s), r3(k_s), cache_kt, k_eff)
    attn_s = _sample_pv(page_table, idx[:, :, :, 0], p_sel, own, r3(v_s), cache_vt,
                        k_eff).reshape(nd, ATTN_WIDTH)
    h_s, w4_s, _ = _sublayer1(xs, pooled_s, attn_s, gp_s, ga_s, sub_f, nd, True)
    y_s = _moe(h_s, w4_s, weg, weu, wed, g2, b2, nd).reshape(nd, 1, d)

    heads = lambda a, lead: a.reshape(lead + (N_HEADS, HEAD_DIM))
    pool_prompt = utail[:, HALO_ROWS - POOL_BUF:][None]
    pool_sample = jnp.concatenate([state_pool[layer][:, 1:], u_s[:, None, :]], axis=1)[None]
    rows = lambda a: jnp.transpose(a.reshape(n, N_HEADS, HEAD_DIM, s), (0, 3, 1, 2))[None]
    return (y_p, y_s.reshape(nd, ld, d),
            rows(k_p), rows(v_p), pool_prompt,
            heads(k_s, (1, nd, 1)), heads(v_s, (1, nd, 1)), pool_sample)
```

```python
import functools

import jax
import jax.numpy as jnp
from jax import lax
from jax.experimental import pallas as pl
from jax.experimental.pallas import tpu as pltpu
from jax.experimental.pallas import tpu_sc as plsc

F32 = jnp.float32
BF16 = jnp.bfloat16

POOL_WINDOWS = (2, 4, 8, 16)
POOL_GROUP_WIDTH = 128
POOL_WIDTH = 512
POOL_BUF = 15
HEAD_DIM = 64
N_HEADS = 8
ATTN_WIDTH = 512
ROT_DIM = 16
ROPE_THETA = 500000.0
MOBA_BLOCK = 256
MOBA_TOPK = 3
PAGE_SIZE = 128
N_GROUPS = 4
EXPERTS_PER_GROUP = 8
EXPERT_TOPK = 2
DEPTH = 1
ALPHA = (2 * DEPTH) ** 0.25
LN_EPS = 1e-5
NEG_INF = -1e30

LANES = 128
SUBLANES = 8
HALO_ROWS = 16
VMEM_LIMIT = 56 * 1024 * 1024

ROUTER_LANES = 128
EXPERT_ROW0 = 8
PAIR = 2 * HEAD_DIM
SCORE_LOOKAHEAD = 8
V_AUG = HEAD_DIM + 16
LOG2_E = 1.4426950408889634


def _cparams(sem):
    return pltpu.CompilerParams(dimension_semantics=sem, vmem_limit_bytes=VMEM_LIMIT)


def _dot(a, b):
    return jnp.dot(a, b, preferred_element_type=F32)


def _dot_nt(a, b):
    return lax.dot_general(a, b, (((1,), (1,)), ((), ())), preferred_element_type=F32)


def _split_bf16(x):
    hi = x.astype(BF16)
    lo = (x - hi.astype(F32)).astype(BF16)
    return hi, lo


def _dot3(a, b, dot=_dot):
    ah, al = _split_bf16(a)
    bh, bl = _split_bf16(b)
    return dot(ah, bh) + dot(ah, bl) + dot(al, bh)


def _dot_bf16(a, b):
    return _dot(a.astype(BF16), b.astype(BF16))


def _lane_sums_as_rows(x):
    ones = jnp.ones((SUBLANES, LANES), BF16)
    hi, lo = _split_bf16(x)
    return _dot_nt(ones, hi) + _dot_nt(ones, lo)


def _sigmoid(x):
    return 1.0 / (1.0 + jnp.exp(-x))


def _layer_norm(x, g, b):
    mu = jnp.mean(x, axis=-1, keepdims=True)
    xc = x - mu
    var = jnp.mean(xc * xc, axis=-1, keepdims=True)
    return xc * lax.rsqrt(var + LN_EPS) * g + b


def _rotary_t(zt, cos, sin):
    half = ROT_DIM // 2
    parts = []
    for h in range(N_HEADS):
        r0 = h * HEAD_DIM
        x1 = zt[r0:r0 + half]
        x2 = zt[r0 + half:r0 + ROT_DIM]
        parts += [x1 * cos - x2 * sin, x2 * cos + x1 * sin, zt[r0 + ROT_DIM:r0 + HEAD_DIM]]
    return jnp.concatenate(parts, axis=0)


def _cos_sin(inv, pos_row):
    ang = inv * pos_row
    return jnp.cos(ang), jnp.sin(ang)


def _first_argmax(work, idxf, axis, n):
    mx = jnp.max(work, axis=axis, keepdims=True)
    first = jnp.min(jnp.where(work == mx, idxf, float(n)), axis=axis, keepdims=True)
    return mx, first


def _proj_prompt_kernel(x_ref, wn_ref, wt_ref, inv_ref,
                        pooled_ref, qt_ref, k_ref, kb_ref, kmean_ref, v_ref, vt_ref,
                        gp_ref, ga_ref, utail_ref, halo_ref, *, tm):
    i = pl.program_id(1)
    nblk = tm // MOBA_BLOCK

    @pl.when(i == 0)
    def _():
        halo_ref[...] = jnp.zeros_like(halo_ref)

    xb = x_ref[0].astype(BF16)
    zn = _dot(xb, wn_ref[...])
    zt = _dot_nt(wt_ref[...], xb)

    u = zn[:, :POOL_WIDTH]
    ext = jnp.concatenate([halo_ref[...], u], axis=0)
    pos1 = i * tm + lax.broadcasted_iota(jnp.int32, (tm, POOL_GROUP_WIDTH), 0) + 1
    outs = []
    for g, w in enumerate(POOL_WINDOWS):
        c0 = g * POOL_GROUP_WIDTH
        s = ext[:, c0:c0 + POOL_GROUP_WIDTH]
        k = 1
        while k < w:
            n = s.shape[0]
            s = s[k:] + s[:n - k]
            k *= 2
        s = s[HALO_ROWS + 1 - w:HALO_ROWS + 1 - w + tm]
        cnt = jnp.minimum(pos1, w).astype(F32)
        outs.append(s / cnt - u[:, c0:c0 + POOL_GROUP_WIDTH])
    pooled_ref[0] = jnp.concatenate(outs, axis=1).astype(BF16)
    halo_ref[...] = u[tm - HALO_ROWS:]
    utail_ref[0] = u[tm - HALO_ROWS:]

    vt_f = zn[:, POOL_WIDTH:POOL_WIDTH + ATTN_WIDTH].T
    v_ref[0] = vt_f
    vt = vt_f.astype(BF16)
    gates = _sigmoid(zn[:, POOL_WIDTH + ATTN_WIDTH:])
    d = gp_ref.shape[-1]
    gp_ref[0] = gates[:, :d]
    ga_ref[0] = gates[:, d:]

    pos = (i * tm + lax.broadcasted_iota(jnp.int32, (1, tm), 1)).astype(F32)
    cos, sin = _cos_sin(inv_ref[...], pos)
    qt = _rotary_t(zt[:ATTN_WIDTH], cos, sin) * (HEAD_DIM ** -0.5 * LOG2_E)
    kt_f = _rotary_t(zt[ATTN_WIDTH:], cos, sin)
    k_ref[0] = kt_f
    kn = kt_f.T
    kb_ref[0] = kn.astype(BF16)
    for j in range(nblk):
        sl = slice(j * MOBA_BLOCK, (j + 1) * MOBA_BLOCK)
        qt_ref[0, j] = qt[:, sl]
        ones = jnp.ones((V_AUG - HEAD_DIM, MOBA_BLOCK), BF16)
        vt_ref[0, j] = jnp.concatenate(
            [piece for h in range(N_HEADS)
             for piece in (vt[h * HEAD_DIM:(h + 1) * HEAD_DIM, sl], ones)], axis=0)
        kmean_ref[0, pl.ds(i * nblk + j, 1), :] = (
            jnp.sum(kn[sl], axis=0, keepdims=True) * (1.0 / MOBA_BLOCK))


def _proj_prompt(x, wn, wt, inv, tm):
    n, s, d = x.shape
    nb = s // MOBA_BLOCK
    nblk = tm // MOBA_BLOCK
    tok = lambda c, dt: jax.ShapeDtypeStruct((n, s, c), dt)
    blk = lambda dt: jax.ShapeDtypeStruct((n, nb, ATTN_WIDTH, MOBA_BLOCK), dt)
    tok_spec = lambda c: pl.BlockSpec((1, tm, c), lambda b, i: (b, i, 0))
    blk_spec = pl.BlockSpec((1, nblk, ATTN_WIDTH, MOBA_BLOCK), lambda b, i: (b, i, 0, 0))
    full = lambda a: pl.BlockSpec(a.shape, lambda b, i: (0,) * a.ndim)
    tr = jax.ShapeDtypeStruct((n, ATTN_WIDTH, s), F32)
    tr_spec = pl.BlockSpec((1, ATTN_WIDTH, tm), lambda b, i: (b, 0, i))
    return pl.pallas_call(
        functools.partial(_proj_prompt_kernel, tm=tm),
        grid=(n, s // tm),
        in_specs=[tok_spec(d), full(wn), full(wt), full(inv)],
        out_specs=[tok_spec(POOL_WIDTH), blk_spec, tr_spec, tok_spec(ATTN_WIDTH),
                   pl.BlockSpec((1, nb, ATTN_WIDTH), lambda b, i: (b, 0, 0)),
                   tr_spec,
                   pl.BlockSpec((1, nblk, N_HEADS * V_AUG, MOBA_BLOCK), lambda b, i: (b, i, 0, 0)),
                   tok_spec(d), tok_spec(d),
                   pl.BlockSpec((1, HALO_ROWS, POOL_WIDTH), lambda b, i: (b, 0, 0))],
        out_shape=[tok(POOL_WIDTH, BF16), blk(F32), tr, tok(ATTN_WIDTH, BF16),
                   jax.ShapeDtypeStruct((n, nb, ATTN_WIDTH), F32),
                   tr, jax.ShapeDtypeStruct((n, nb, N_HEADS * V_AUG, MOBA_BLOCK), BF16),
                   tok(d, F32), tok(d, F32),
                   jax.ShapeDtypeStruct((n, HALO_ROWS, POOL_WIDTH), F32)],
        scratch_shapes=[pltpu.VMEM((HALO_ROWS, POOL_WIDTH), F32)],
        compiler_params=_cparams(("arbitrary", "arbitrary")),
        name="proj_prompt",
    )(x, wn, wt, inv)


def _proj_sample_kernel(x_ref, wn_ref, wt_ref, inv_ref, st_ref,
                        pooled_ref, u_ref, q_ref, k_ref, v_ref, gp_ref, ga_ref, *, past_len):
    x = x_ref[...]
    uv_w = POOL_WIDTH + ATTN_WIDTH
    zn = _dot3(x, wn_ref[:, :uv_w])
    zt = _dot3(wt_ref[...], x, _dot_nt)
    m = x.shape[0]

    u = zn[:, :POOL_WIDTH]
    u_ref[...] = u
    outs = []
    for g, w in enumerate(POOL_WINDOWS):
        c0 = g * POOL_GROUP_WIDTH
        s = u[:, c0:c0 + POOL_GROUP_WIDTH]
        for r in range(POOL_BUF - (w - 1), POOL_BUF):
            s = s + st_ref[r, :, c0:c0 + POOL_GROUP_WIDTH]
        cnt = float(min(past_len + 1, w))
        outs.append(s / cnt - u[:, c0:c0 + POOL_GROUP_WIDTH])
    pooled_ref[...] = jnp.concatenate(outs, axis=1)

    v_ref[...] = zn[:, POOL_WIDTH:]
    gates = _sigmoid(_dot_bf16(x, wn_ref[:, uv_w:]))
    d = gp_ref.shape[-1]
    gp_ref[...] = gates[:, :d]
    ga_ref[...] = gates[:, d:]

    pos = jnp.full((1, m), float(past_len), F32)
    cos, sin = _cos_sin(inv_ref[...], pos)
    q_ref[...] = (_rotary_t(zt[:ATTN_WIDTH], cos, sin) * (HEAD_DIM ** -0.5)).T
    k_ref[...] = _rotary_t(zt[ATTN_WIDTH:], cos, sin).T


def _proj_sample(x, wn, wt, inv, state_t, past_len):
    m, d = x.shape
    o = lambda c: jax.ShapeDtypeStruct((m, c), F32)
    return pl.pallas_call(
        functools.partial(_proj_sample_kernel, past_len=past_len),
        out_shape=[o(POOL_WIDTH), o(POOL_WIDTH), o(ATTN_WIDTH), o(ATTN_WIDTH),
                   o(ATTN_WIDTH), o(d), o(d)],
        compiler_params=pltpu.CompilerParams(vmem_limit_bytes=VMEM_LIMIT),
        name="proj_sample",
    )(x, wn, wt, inv, state_t)


def _moba_prompt_kernel(qt_ref, kb_ref, vt_ref, kmean_ref, out_ref,
                        sel_ref, qbd_ref, m_ref, acc_ref):
    blk = pl.program_id(1)
    nb = kmean_ref.shape[1]
    npair = ATTN_WIDTH // PAIR
    two = 2 * MOBA_BLOCK
    rows = lax.broadcasted_iota(jnp.int32, (PAIR, MOBA_BLOCK), 0)
    jj = lax.broadcasted_iota(jnp.int32, (nb, two), 0)
    jf = jj.astype(F32)
    valid = jj < blk

    for p in range(npair):
        ps = slice(p * PAIR, (p + 1) * PAIR)
        qt = qt_ref[0, 0, ps, :]
        qbd = jnp.concatenate([jnp.where(rows < HEAD_DIM, qt, 0.0),
                               jnp.where(rows >= HEAD_DIM, qt, 0.0)], axis=1)
        qbd_ref[p] = qbd.astype(BF16)
        work = jnp.where(valid, _dot3(kmean_ref[0, :, ps], qbd), NEG_INF)
        sel = jnp.zeros((nb, two), jnp.bool_)
        for _ in range(min(MOBA_TOPK, nb - 1)):
            _, first = _first_argmax(work, jf, 0, nb)
            pick = jf == first
            sel = jnp.logical_or(sel, pick)
            work = jnp.where(pick, -jnp.inf, work)
        sel_ref[p] = jnp.where(jnp.logical_and(sel, valid), 1.0, 0.0)
    for h in range(N_HEADS):
        m_ref[h] = jnp.full((1, MOBA_BLOCK), NEG_INF, F32)
        acc_ref[h] = jnp.zeros((V_AUG, MOBA_BLOCK), F32)

    half = MOBA_BLOCK // 2
    kpos = lax.broadcasted_iota(jnp.int32, (half, MOBA_BLOCK), 0)
    qpos = lax.broadcasted_iota(jnp.int32, (half, MOBA_BLOCK), 1)

    def attend(j, past):
        row0 = pl.multiple_of(j * MOBA_BLOCK, MOBA_BLOCK)

        def scores(h):
            p, a = divmod(h, 2)
            out = []
            for kh in range(2):
                kt = kb_ref[0, pl.ds(row0 + kh * half, half), p * PAIR:(p + 1) * PAIR]
                st = _dot(kt, qbd_ref[p, :, a * MOBA_BLOCK:(a + 1) * MOBA_BLOCK])
                if not past:
                    st = jnp.where(kpos + kh * half <= qpos, st, NEG_INF)
                out.append(st)
            return out

        sts = {h: scores(h) for h in range(min(SCORE_LOOKAHEAD, N_HEADS))}
        for h in range(N_HEADS):
            if h + SCORE_LOOKAHEAD < N_HEADS:
                sts[h + SCORE_LOOKAHEAD] = scores(h + SCORE_LOOKAHEAD)
            p, a = divmod(h, 2)
            qs = slice(a * MOBA_BLOCK, (a + 1) * MOBA_BLOCK)
            st0, st1 = sts.pop(h)
            m = m_ref[h]
            mj = jnp.maximum(jnp.max(st0, axis=0, keepdims=True), jnp.max(st1, axis=0, keepdims=True))
            if past:
                on = sel_ref[p, pl.ds(j, 1), qs]
                m_new = jnp.where(on > 0.0, jnp.maximum(m, mj), m)
                shift = jnp.where(on > 0.0, m_new, mj)
            else:
                m_new = jnp.maximum(m, mj)
                shift = m_new
            pr0 = jnp.exp2(st0 - shift)
            pr1 = jnp.exp2(st1 - shift)
            alpha = jnp.exp2(m - m_new)
            hs = slice(h * V_AUG, (h + 1) * V_AUG)
            o = (_dot(vt_ref[0, j, hs, :half], pr0.astype(BF16))
                 + _dot(vt_ref[0, j, hs, half:], pr1.astype(BF16)))
            if past:
                o = on * o
            m_ref[h] = m_new
            acc_ref[h] = alpha * acc_ref[h] + o

    def past_block(j, carry):
        attend(j, True)
        return carry

    lax.fori_loop(0, blk, past_block, 0)
    attend(blk, False)

    def normalized(h):
        acc = acc_ref[h]
        return acc[:HEAD_DIM] / acc[HEAD_DIM:HEAD_DIM + 1]

    for p in range(npair):
        o = jnp.concatenate([normalized(2 * p), normalized(2 * p + 1)], axis=0)
        out_ref[0, :, p * PAIR:(p + 1) * PAIR] = o.T


def _moba_prompt(qt, kb, vt, kmean):
    n, nb, _, _ = qt.shape
    s = kb.shape[1]
    npair = ATTN_WIDTH // PAIR
    return pl.pallas_call(
        _moba_prompt_kernel,
        grid=(n, nb),
        in_specs=[pl.BlockSpec((1, 1, ATTN_WIDTH, MOBA_BLOCK), lambda b, j: (b, j, 0, 0)),
                  pl.BlockSpec((1, s, ATTN_WIDTH), lambda b, j: (b, 0, 0)),
                  pl.BlockSpec((1, nb, N_HEADS * V_AUG, MOBA_BLOCK), lambda b, j: (b, 0, 0, 0)),
                  pl.BlockSpec((1, nb, ATTN_WIDTH), lambda b, j: (b, 0, 0))],
        out_specs=pl.BlockSpec((1, MOBA_BLOCK, ATTN_WIDTH), lambda b, j: (b, j, 0)),
        out_shape=jax.ShapeDtypeStruct((n, s, ATTN_WIDTH), F32),
        scratch_shapes=[pltpu.VMEM((npair, nb, 2 * MOBA_BLOCK), F32),
                        pltpu.VMEM((npair, PAIR, 2 * MOBA_BLOCK), BF16),
                        pltpu.VMEM((N_HEADS, 1, MOBA_BLOCK), F32),
                        pltpu.VMEM((N_HEADS, V_AUG, MOBA_BLOCK), F32)],
        compiler_params=_cparams(("arbitrary", "arbitrary")),
        name="moba_prompt",
    )(qt, kb, vt, kmean)


def _sublayer1_kernel(x_ref, pooled_ref, attn_ref, gp_ref, ga_ref, wg_ref, scale_ref, wpp_ref, wap_ref,
                      wout_ref, g1_ref, b1_ref, wr_ref, br_ref, h_ref, w4_ref, gsel_ref, *, precise):
    tm = x_ref.shape[0]
    mm = _dot3 if precise else _dot_bf16
    pooled = pooled_ref[...]
    py = jnp.concatenate(
        [mm(pooled[:, g * POOL_GROUP_WIDTH:(g + 1) * POOL_GROUP_WIDTH], wg_ref[g])
         for g in range(len(POOL_WINDOWS))], axis=1) * scale_ref[...]
    a = mm(py, wpp_ref[...])
    b = mm(attn_ref[...], wap_ref[...])
    merged = gp_ref[...] * a + ga_ref[...] * b
    h = _layer_norm(ALPHA * x_ref[...] + mm(merged, wout_ref[...]), g1_ref[...], b1_ref[...])
    h_ref[...] = h

    lt = (_dot3(h, wr_ref[...]) + br_ref[...]).T
    idx = lax.broadcasted_iota(jnp.int32, (SUBLANES, tm), 0)
    idf = idx.astype(F32)
    gl = jnp.where(idx < N_GROUPS, lt[:SUBLANES], -jnp.inf)
    gmax, gsel = _first_argmax(gl, idf, 0, SUBLANES)
    gsel_ref[0] = gsel
    gexp = jnp.exp(gl - gmax)
    gprob = gexp / jnp.sum(gexp, axis=0, keepdims=True)
    pg = jnp.sum(jnp.where(idf == gsel, gprob, 0.0), axis=0, keepdims=True)
    elog = jnp.zeros((SUBLANES, tm), F32)
    for g in range(N_GROUPS):
        r0 = EXPERT_ROW0 + g * EXPERTS_PER_GROUP
        elog = jnp.where(gsel == float(g), lt[r0:r0 + EXPERTS_PER_GROUP], elog)
    e1, i1 = _first_argmax(elog, idf, 0, SUBLANES)
    e2, i2 = _first_argmax(jnp.where(idf == i1, -jnp.inf, elog), idf, 0, SUBLANES)
    x2 = jnp.exp(e2 - e1)
    den = 1.0 + x2
    w_e = jnp.where(idf == i1, 1.0 / den, 0.0) + jnp.where(idf == i2, x2 / den, 0.0)
    pw = pg * w_e
    slabs = [jnp.where(gsel == float(g), pw, 0.0) for g in range(N_GROUPS)]
    slabs.append(jnp.zeros((ROUTER_LANES - N_GROUPS * EXPERTS_PER_GROUP, tm), F32))
    wn = jnp.concatenate(slabs, axis=0).T
    for g in range(N_GROUPS):
        sh = (ROUTER_LANES - g * EXPERTS_PER_GROUP) % ROUTER_LANES
        w4_ref[g] = wn if sh == 0 else pltpu.roll(wn, sh, 1)


def _sublayer1(x, pooled, attn, gp, ga, wts, tm, precise):
    t, d = x.shape
    row = lambda c: pl.BlockSpec((tm, c), lambda i: (i, 0))
    full = lambda a: pl.BlockSpec(a.shape, lambda i: (0,) * a.ndim)
    return pl.pallas_call(
        functools.partial(_sublayer1_kernel, precise=precise),
        grid=(t // tm,),
        in_specs=[row(d), row(POOL_WIDTH), row(ATTN_WIDTH), row(d), row(d)] + [full(a) for a in wts],
        out_specs=[row(d), pl.BlockSpec((N_GROUPS, tm, ROUTER_LANES), lambda i: (0, i, 0)),
                   pl.BlockSpec((1, 1, tm), lambda i: (i, 0, 0))],
        out_shape=[jax.ShapeDtypeStruct((t, d), F32),
                   jax.ShapeDtypeStruct((N_GROUPS, t, ROUTER_LANES), F32),
                   jax.ShapeDtypeStruct((t // tm, 1, tm), F32)],
        compiler_params=_cparams(("arbitrary",)),
        name="sublayer1",
    )(x, pooled, attn, gp, ga, *wts)


def _moe_kernel(h_ref, w_ref, weg_ref, weu_ref, wed_ref, g2_ref, b2_ref, y_ref, acc_ref):
    g = pl.program_id(1)

    @pl.when(g == 0)
    def _():
        acc_ref[...] = jnp.zeros_like(acc_ref)

    h = h_ref[...]
    hb = h.astype(BF16)
    w = w_ref[0]
    hid = []
    for e in range(EXPERTS_PER_GROUP):
        a = _dot(hb, weg_ref[0, e])
        b = _dot(hb, weu_ref[0, e])
        hid.append((a * _sigmoid(a) * b * w[:, e:e + 1]).astype(BF16))
    acc_ref[...] += _dot(jnp.concatenate(hid, axis=1), wed_ref[0])

    @pl.when(g == pl.num_programs(1) - 1)
    def _():
        y_ref[...] = _layer_norm(ALPHA * h + acc_ref[...], g2_ref[...], b2_ref[...])


def _moe(h, w4, weg, weu, wed, g2, b2, tm):
    t, d = h.shape
    ng, ne, _, f = weg.shape
    return pl.pallas_call(
        _moe_kernel,
        grid=(t // tm, ng),
        in_specs=[pl.BlockSpec((tm, d), lambda i, g: (i, 0)),
                  pl.BlockSpec((1, tm, ROUTER_LANES), lambda i, g: (g, i, 0)),
                  pl.BlockSpec((1, ne, d, f), lambda i, g: (g, 0, 0, 0)),
                  pl.BlockSpec((1, ne, d, f), lambda i, g: (g, 0, 0, 0)),
                  pl.BlockSpec((1, ne * f, d), lambda i, g: (g, 0, 0)),
                  pl.BlockSpec(g2.shape, lambda i, g: (0, 0)),
                  pl.BlockSpec(b2.shape, lambda i, g: (0, 0))],
        out_specs=pl.BlockSpec((tm, d), lambda i, g: (i, 0)),
        out_shape=jax.ShapeDtypeStruct((t, d), F32),
        scratch_shapes=[pltpu.VMEM((tm, d), F32)],
        compiler_params=_cparams(("arbitrary", "arbitrary")),
        name="moe",
    )(h, w4, weg, weu, wed, g2, b2)


MOE_CAP = 320


def _lane_prefix_sum(x):
    n = x.shape[1]
    lane = lax.broadcasted_iota(jnp.int32, x.shape, 1)
    s = 1
    while s < n:
        x = x + jnp.where(lane >= s, pltpu.roll(x, s, 1), 0.0)
        s *= 2
    return x


def _moe_grouped_kernel(h_ref, gsel_ref, w_ref, weg_ref, weu_ref, wed_ref, g2_ref, b2_ref, y_ref,
                        hb_ref, acc_ref, rank_ref, rankt_ref):
    g = pl.program_id(1)
    tm = h_ref.shape[0]
    cap = MOE_CAP

    @pl.when(g == 0)
    def _():
        hb_ref[...] = h_ref[...].astype(BF16)
        acc_ref[...] = jnp.zeros_like(acc_ref)
        sub = lax.broadcasted_iota(jnp.int32, (SUBLANES, tm), 0).astype(F32)
        member = jnp.where(sub == gsel_ref[0], 1.0, 0.0)
        rank = jnp.where(member > 0.0, _lane_prefix_sum(member) - 1.0, -1.0)
        rank_ref[...] = rank
        rank_t = jnp.concatenate([rank, jnp.full((LANES - SUBLANES, tm), -1.0, F32)], axis=0).T
        for gg in range(N_GROUPS):
            rankt_ref[gg] = jnp.broadcast_to(rank_t[:, gg:gg + 1], (tm, LANES))

    rrow = rank_ref[pl.ds(g, 1), :]
    rcol = rankt_ref[g]
    wh, wl = _split_bf16(w_ref[0])
    count = jnp.max(rrow).astype(jnp.int32) + 1
    n_pass = (count + (cap - 1)) // cap
    sub_i = lax.broadcasted_iota(jnp.int32, (cap, tm), 0).astype(F32)
    lane_i = lax.broadcasted_iota(jnp.int32, (tm, LANES), 1).astype(F32)

    def one_pass(ps, carry):
        base = (ps * cap).astype(F32)
        pick = jnp.where(rrow - base == sub_i, 1.0, 0.0).astype(BF16)
        xg = _dot(pick, hb_ref[...]).astype(BF16)
        wg = _dot(pick, wh) + _dot(pick, wl)
        hid = []
        for e in range(EXPERTS_PER_GROUP):
            a = _dot(xg, weg_ref[0, e])
            b = _dot(xg, weu_ref[0, e])
            hid.append((a * _sigmoid(a) * b * wg[:, e:e + 1]).astype(BF16))
        yg = _dot(jnp.concatenate(hid, axis=1), wed_ref[0]).astype(BF16)
        cap_pad = -(-cap // LANES) * LANES
        if cap_pad > cap:
            yg = jnp.concatenate([yg, jnp.zeros((cap_pad - cap, yg.shape[1]), BF16)], axis=0)
        put = jnp.concatenate(
            [jnp.where(rcol - base == lane_i + float(k * LANES), 1.0, 0.0) for k in range(cap_pad // LANES)],
            axis=1).astype(BF16)
        acc_ref[...] += _dot(put, yg)
        return carry

    lax.fori_loop(0, n_pass, one_pass, 0)

    @pl.when(g == pl.num_programs(1) - 1)
    def _():
        y_ref[...] = _layer_norm(ALPHA * h_ref[...] + acc_ref[...], g2_ref[...], b2_ref[...])


def _moe_grouped(h, gsel, w4, weg, weu, wed, g2, b2, tm):
    t, d = h.shape
    ng, ne, _, f = weg.shape
    gsel = gsel.reshape(t // tm, 1, tm)
    return pl.pallas_call(
        _moe_grouped_kernel,
        grid=(t // tm, ng),
        in_specs=[pl.BlockSpec((tm, d), lambda i, g: (i, 0)),
                  pl.BlockSpec((1, 1, tm), lambda i, g: (i, 0, 0)),
                  pl.BlockSpec((1, tm, ROUTER_LANES), lambda i, g: (g, i, 0)),
                  pl.BlockSpec((1, ne, d, f), lambda i, g: (g, 0, 0, 0)),
                  pl.BlockSpec((1, ne, d, f), lambda i, g: (g, 0, 0, 0)),
                  pl.BlockSpec((1, ne * f, d), lambda i, g: (g, 0, 0)),
                  pl.BlockSpec(g2.shape, lambda i, g: (0, 0)),
                  pl.BlockSpec(b2.shape, lambda i, g: (0, 0))],
        out_specs=pl.BlockSpec((tm, d), lambda i, g: (i, 0)),
        out_shape=jax.ShapeDtypeStruct((t, d), F32),
        scratch_shapes=[pltpu.VMEM((tm, d), BF16), pltpu.VMEM((tm, d), F32),
                        pltpu.VMEM((SUBLANES, tm), F32), pltpu.VMEM((N_GROUPS, tm, LANES), F32)],
        compiler_params=_cparams(("arbitrary", "arbitrary")),
        name="moe_grouped",
    )(h, gsel, w4, weg, weu, wed, g2, b2)


def _lane_rep(row):
    return jnp.broadcast_to(row, (LANES, ATTN_WIDTH)).T.reshape(N_HEADS, HEAD_DIM, LANES)


SC_LANES = 16
SC_WORKERS = 32
SC_ROWS = 32
SC_CHUNK = 8


def _sc_logits(cache_rows, row_idx, q_rep, nseq, n_pages):
    per_w = nseq // SC_WORKERS
    chunks = n_pages * 2
    heads_per_chunk = N_HEADS // 2
    rows_per_head = HEAD_DIM // SC_ROWS
    qrows = N_HEADS * HEAD_DIM // SUBLANES
    mesh = plsc.VectorSubcoreMesh(core_axis_name="c", subcore_axis_name="s")

    @functools.partial(
        pl.kernel, mesh=mesh,
        out_type=jax.ShapeDtypeStruct((nseq * n_pages, N_HEADS, PAGE_SIZE), F32),
        scratch_types=[pltpu.VMEM((chunks * SC_CHUNK,), jnp.int32),
                       pltpu.VMEM((qrows, LANES), F32),
                       pltpu.VMEM((2, SC_CHUNK, SC_ROWS, PAGE_SIZE), F32),
                       pltpu.VMEM((N_HEADS, PAGE_SIZE), F32),
                       pltpu.SemaphoreType.DMA((2,))],
    )
    def k(ck_hbm, idx_hbm, q_hbm, out_hbm, idx_v, q_v, kbuf, obuf, sem):
        wid = lax.axis_index("s") * 2 + lax.axis_index("c")

        def chunk_copy(ci, b):
            return pltpu.make_async_copy(ck_hbm.at[idx_v.at[pl.ds(ci * SC_CHUNK, SC_CHUNK)]],
                                         kbuf.at[b], sem.at[b])

        @pl.loop(0, per_w)
        def _(t):
            n = wid * per_w + t
            pltpu.sync_copy(idx_hbm.at[pl.ds(n * (chunks * SC_CHUNK), chunks * SC_CHUNK)], idx_v)
            pltpu.sync_copy(q_hbm.at[n], q_v)
            chunk_copy(0, 0).start()

            @pl.loop(0, n_pages)
            def _(i):
                for g in range(2):
                    ci = i * 2 + g

                    @pl.when(ci + 1 < chunks)
                    def _():
                        chunk_copy(ci + 1, 1 - g).start()

                    chunk_copy(ci, g).wait()
                    @pl.loop(0, heads_per_chunk)
                    def _(hl):
                        h = g * heads_per_chunk + hl
                        accs = [jnp.zeros((SC_LANES,), F32) for _ in range(PAGE_SIZE // SC_LANES)]
                        for r in range(rows_per_head):
                            for dd in range(SC_ROWS):
                                d8, j = divmod(dd, SUBLANES)
                                qv = q_v[h * (HEAD_DIM // SUBLANES) + r * (SC_ROWS // SUBLANES) + d8,
                                         pl.ds(j * SC_LANES, SC_LANES)]
                                for c in range(PAGE_SIZE // SC_LANES):
                                    kv = kbuf[g, hl * rows_per_head + r, dd, pl.ds(c * SC_LANES, SC_LANES)]
                                    accs[c] = accs[c] + qv * kv
                        for c in range(PAGE_SIZE // SC_LANES):
                            obuf[h, pl.ds(c * SC_LANES, SC_LANES)] = accs[c]
                pltpu.sync_copy(obuf, out_hbm.at[n * n_pages + i])

    return k(cache_rows, row_idx, q_rep)


def _sample_select_kernel(lg_ref, q_ref, kn_ref, p_ref, idx_ref, own_ref, *, k_eff):
    ppb = MOBA_BLOCK // PAGE_SIZE
    x = lg_ref[0]
    nblk = x.shape[0] // ppb
    x = x.reshape(nblk, ppb, N_HEADS, PAGE_SIZE)
    logits = jnp.concatenate([x[:, j] for j in range(ppb)], axis=-1)
    work = jnp.sum(logits, axis=2, keepdims=True) * (1.0 / MOBA_BLOCK)
    bf = lax.broadcasted_iota(jnp.int32, work.shape, 0).astype(F32)
    sel = jnp.zeros(work.shape, jnp.bool_)
    for k in range(k_eff):
        _, first = _first_argmax(work, bf, 0, nblk)
        idx_ref[0, k] = jnp.broadcast_to(first[0], (N_HEADS, LANES)).astype(jnp.int32)
        pick = bf == first
        sel = jnp.logical_or(sel, pick)
        work = jnp.where(pick, -jnp.inf, work)

    s_own = jnp.sum(_lane_rep(kn_ref[0]) * _lane_rep(q_ref[0]), axis=1)[None, :, :1]
    lm = jnp.where(sel, logits, NEG_INF)
    m = jnp.maximum(jnp.max(jnp.max(lm, axis=2, keepdims=True), axis=0, keepdims=True), s_own)
    p = jnp.exp(lm - m)
    p_own = jnp.exp(s_own - m)
    l = jnp.sum(jnp.sum(p, axis=2, keepdims=True), axis=0, keepdims=True) + p_own
    p_ref[0] = p / l
    own_ref[0] = jnp.broadcast_to((p_own / l)[0], (N_HEADS, LANES))


def _sample_select(logits, q3, kn3, k_eff):
    nseq, n_pages = logits.shape[:2]
    nblk = n_pages * PAGE_SIZE // MOBA_BLOCK
    row_spec = pl.BlockSpec((1, 1, ATTN_WIDTH), lambda n: (n, 0, 0))
    return pl.pallas_call(
        functools.partial(_sample_select_kernel, k_eff=k_eff),
        grid=(nseq,),
        in_specs=[pl.BlockSpec((1, n_pages, N_HEADS, PAGE_SIZE), lambda n: (n, 0, 0, 0)),
                  row_spec, row_spec],
        out_specs=[pl.BlockSpec((1, nblk, N_HEADS, MOBA_BLOCK), lambda n: (n, 0, 0, 0)),
                   pl.BlockSpec((1, k_eff, N_HEADS, LANES), lambda n: (n, 0, 0, 0)),
                   pl.BlockSpec((1, N_HEADS, LANES), lambda n: (n, 0, 0))],
        out_shape=[jax.ShapeDtypeStruct((nseq, nblk, N_HEADS, MOBA_BLOCK), F32),
                   jax.ShapeDtypeStruct((nseq, k_eff, N_HEADS, LANES), jnp.int32),
                   jax.ShapeDtypeStruct((nseq, N_HEADS, LANES), F32)],
        compiler_params=_cparams(("arbitrary",)),
        name="sample_select",
    )(logits, q3, kn3)


def _sample_pv_kernel(pt_ref, idx_ref, p_ref, own_ref, vn_ref, cv_ref, out_ref,
                      vbuf_ref, sem_ref, *, k_eff):
    n = pl.program_id(0)
    nseq = pl.num_programs(0)
    ppb = MOBA_BLOCK // PAGE_SIZE

    def copies(seq, slot):
        out = []
        for h in range(N_HEADS):
            for k in range(k_eff):
                blk = idx_ref[seq, k, h]
                for j in range(ppb):
                    page = pt_ref[seq, blk * ppb + j]
                    out.append(pltpu.make_async_copy(
                        cv_ref.at[page, h], vbuf_ref.at[slot, h, k * ppb + j], sem_ref.at[slot]))
        return out

    slot = n % 2

    @pl.when(n == 0)
    def _():
        for c in copies(0, 0):
            c.start()

    @pl.when(n + 1 < nseq)
    def _():
        for c in copies(n + 1, 1 - slot):
            c.start()

    for c in copies(n, slot):
        c.wait()

    accs = []
    for h in range(N_HEADS):
        acc = jnp.zeros((HEAD_DIM, PAGE_SIZE), F32)
        for k in range(k_eff):
            pk = p_ref[0, idx_ref[n, k, h], h:h + 1, :]
            for j in range(ppb):
                acc = acc + pk[:, j * PAGE_SIZE:(j + 1) * PAGE_SIZE] * vbuf_ref[slot, h, k * ppb + j]
        accs.append(acc)
    own = own_ref[0]
    accs += [jnp.broadcast_to(own[h:h + 1], (HEAD_DIM, LANES)) for h in range(N_HEADS)]
    r = _lane_sums_as_rows(jnp.concatenate(accs, axis=0))
    out_ref[0] = r[0:1, :ATTN_WIDTH] + r[0:1, ATTN_WIDTH:] * (1.0 / LANES) * vn_ref[0]


def _sample_pv(page_table, idx, p, own, vn3, cache_vt, k_eff):
    nseq = vn3.shape[0]
    nblk = p.shape[1]
    ppb = MOBA_BLOCK // PAGE_SIZE
    row_spec = pl.BlockSpec((1, 1, ATTN_WIDTH), lambda n, pt, ix: (n, 0, 0))
    return pl.pallas_call(
        functools.partial(_sample_pv_kernel, k_eff=k_eff),
        grid_spec=pltpu.PrefetchScalarGridSpec(
            num_scalar_prefetch=2,
            grid=(nseq,),
            in_specs=[pl.BlockSpec((1, nblk, N_HEADS, MOBA_BLOCK), lambda n, pt, ix: (n, 0, 0, 0)),
                      pl.BlockSpec((1, N_HEADS, LANES), lambda n, pt, ix: (n, 0, 0)),
                      row_spec, pl.BlockSpec(memory_space=pl.ANY)],
            out_specs=row_spec,
            scratch_shapes=[pltpu.VMEM((2, N_HEADS, k_eff * ppb, HEAD_DIM, PAGE_SIZE), F32),
                            pltpu.SemaphoreType.DMA((2,))]),
        out_shape=jax.ShapeDtypeStruct((nseq, 1, ATTN_WIDTH), F32),
        compiler_params=_cparams(("arbitrary",)),
        name="sample_pv",
    )(page_table, idx, p, own, vn3, cache_vt)


def kernel(x_prompt, x_sample, cache_k, cache_v, state_pool, page_table, w_in, w_pool_group,
           pool_scale, w_pool_proj, w_attn_proj, w_out, ln1_g, ln1_b, w_router_group,
           b_router_group, w_router_expert, b_router_expert, w_exp_gate, w_exp_up,
           w_exp_down, ln2_g, ln2_b):
    assert w_in.shape[0] == DEPTH == 1
    n, s, d = x_prompt.shape
    nd, ld, _ = x_sample.shape
    assert ld == 1
    past_len = page_table.shape[1] * PAGE_SIZE
    assert past_len % MOBA_BLOCK == 0 and past_len // MOBA_BLOCK >= MOBA_TOPK
    layer = 0

    a0 = POOL_WIDTH
    win = w_in[layer]
    wn_f = jnp.concatenate([win[:, :a0], win[:, a0 + 2 * ATTN_WIDTH:]], axis=1)
    wt_f = win[:, a0:a0 + 2 * ATTN_WIDTH].T
    wn, wt = wn_f.astype(BF16), wt_f.astype(BF16)
    half = ROT_DIM // 2
    inv = (ROPE_THETA ** (-2.0 * jnp.arange(half, dtype=F32) / ROT_DIM)).reshape(half, 1)
    row = lambda a: a.reshape(1, -1)
    ng, ne = N_GROUPS, EXPERTS_PER_GROUP
    wr = jnp.zeros((d, ROUTER_LANES), F32)
    wr = wr.at[:, :ng].set(w_router_group[layer])
    wr = wr.at[:, EXPERT_ROW0:EXPERT_ROW0 + ng * ne].set(
        jnp.transpose(w_router_expert[layer], (1, 0, 2)).reshape(d, ng * ne))
    br = jnp.zeros((1, ROUTER_LANES), F32)
    br = br.at[0, :ng].set(b_router_group[layer])
    br = br.at[0, EXPERT_ROW0:EXPERT_ROW0 + ng * ne].set(b_router_expert[layer].reshape(-1))
    sub_f = (w_pool_group[layer], row(pool_scale[layer]), w_pool_proj[layer], w_attn_proj[layer],
             w_out[layer], row(ln1_g[layer]), row(ln1_b[layer]), wr, br)
    cast = (0, 2, 3, 4)
    sub_b = tuple(a.astype(BF16) if i in cast else a for i, a in enumerate(sub_f))
    weg = w_exp_gate[layer].astype(BF16)
    weu = w_exp_up[layer].astype(BF16)
    f = weg.shape[-1]
    wed = w_exp_down[layer].astype(BF16).reshape(ng, ne * f, d)
    g2, b2 = row(ln2_g[layer]), row(ln2_b[layer])

    tm = 512
    pooled, qt, k_p, kb, kmean, v_p, vt, gp, ga, utail = _proj_prompt(x_prompt, wn, wt, inv, tm)
    attn = _moba_prompt(qt, kb, vt, kmean)
    t = n * s
    flat = lambda a: a.reshape(t, a.shape[-1])
    h_p, w4_p, gsel_p = _sublayer1(flat(x_prompt), flat(pooled), flat(attn), flat(gp), flat(ga),
                                   sub_b, tm, False)
    y_p = _moe_grouped(h_p, gsel_p, w4_p, weg, weu, wed, g2, b2, 2 * tm).reshape(n, s, d)

    xs = x_sample.reshape(nd, d)
    state_t = jnp.transpose(state_pool[layer], (1, 0, 2))
    pooled_s, u_s, q_s, k_s, v_s, gp_s, ga_s = _proj_sample(xs, wn_f, wt_f, inv, state_t, past_len)
    cache_kt = jnp.transpose(cache_k[layer], (0, 2, 3, 1))
    cache_vt = jnp.transpose(cache_v[layer], (0, 2, 3, 1))
    r3 = lambda a: a.reshape(nd, 1, a.shape[-1])
    k_eff = MOBA_TOPK
    assert nd % SC_WORKERS == 0
    n_pages = page_table.shape[1]
    n_pool = cache_k.shape[1]
    cache_rows = cache_kt.reshape(n_pool * 2 * SC_CHUNK, SC_ROWS, PAGE_SIZE)
    row_idx = (page_table[:, :, None] * (2 * SC_CHUNK)
               + jnp.arange(2 * SC_CHUNK, dtype=jnp.int32)[None, None, :]).reshape(-1)
    q_rep = jnp.broadcast_to(q_s.reshape(nd, ATTN_WIDTH // SUBLANES, SUBLANES, 1),
                             (nd, ATTN_WIDTH // SUBLANES, SUBLANES, SC_LANES)
                             ).reshape(nd, ATTN_WIDTH // SUBLANES, LANES)
    logits = _sc_logits(cache_rows, row_idx, q_rep, nd, n_pages).reshape(
        nd, n_pages, N_HEADS, PAGE_SIZE)
    p_sel, idx, own = _sample_select(logits, r3(q_s), r3(k_s), k_eff)
    attn_s = _sample_pv(page_table, idx[:, :, :, 0], p_sel, own, r3(v_s), cache_vt,
                        k_eff).reshape(nd, ATTN_WIDTH)
    h_s, w4_s, _ = _sublayer1(xs, pooled_s, attn_s, gp_s, ga_s, sub_f, nd, True)
    y_s = _moe(h_s, w4_s, weg, weu, wed, g2, b2, nd).reshape(nd, 1, d)

    heads = lambda a, lead: a.reshape(lead + (N_HEADS, HEAD_DIM))
    pool_prompt = utail[:, HALO_ROWS - POOL_BUF:][None]
    pool_sample = jnp.concatenate([state_pool[layer][:, 1:], u_s[:, None, :]], axis=1)[None]
    rows = lambda a: jnp.transpose(a.reshape(n, N_HEADS, HEAD_DIM, s), (0, 3, 1, 2))[None]
    return (y_p, y_s.reshape(nd, ld, d),
            rows(k_p), rows(v_p), pool_prompt,
            heads(k_s, (1, nd, 1)), heads(v_s, (1, nd, 1)), pool_sample)
```

```python
import functools

import jax
import jax.numpy as jnp
from jax import lax
from jax.experimental import pallas as pl
from jax.experimental.pallas import tpu as pltpu
from jax.experimental.pallas import tpu_sc as plsc

F32 = jnp.float32
BF16 = jnp.bfloat16

POOL_WINDOWS = (2, 4, 8, 16)
POOL_GROUP_WIDTH = 128
POOL_WIDTH = 512
POOL_BUF = 15
HEAD_DIM = 64
N_HEADS = 8
ATTN_WIDTH = 512
ROT_DIM = 16
ROPE_THETA = 500000.0
MOBA_BLOCK = 256
MOBA_TOPK = 3
PAGE_SIZE = 128
N_GROUPS = 4
EXPERTS_PER_GROUP = 8
EXPERT_TOPK = 2
DEPTH = 1
ALPHA = (2 * DEPTH) ** 0.25
LN_EPS = 1e-5
NEG_INF = -1e30

LANES = 128
SUBLANES = 8
HALO_ROWS = 16
VMEM_LIMIT = 56 * 1024 * 1024

ROUTER_LANES = 128
EXPERT_ROW0 = 8
PAIR = 2 * HEAD_DIM
SCORE_LOOKAHEAD = 8
V_AUG = HEAD_DIM + 16
LOG2_E = 1.4426950408889634


def _cparams(sem):
    return pltpu.CompilerParams(dimension_semantics=sem, vmem_limit_bytes=VMEM_LIMIT)


def _dot(a, b):
    return jnp.dot(a, b, preferred_element_type=F32)


def _dot_nt(a, b):
    return lax.dot_general(a, b, (((1,), (1,)), ((), ())), preferred_element_type=F32)


def _split_bf16(x):
    hi = x.astype(BF16)
    lo = (x - hi.astype(F32)).astype(BF16)
    return hi, lo


def _dot3(a, b, dot=_dot):
    ah, al = _split_bf16(a)
    bh, bl = _split_bf16(b)
    return dot(ah, bh) + dot(ah, bl) + dot(al, bh)


def _dot_bf16(a, b):
    return _dot(a.astype(BF16), b.astype(BF16))


def _lane_sums_as_rows(x):
    ones = jnp.ones((SUBLANES, LANES), BF16)
    hi, lo = _split_bf16(x)
    return _dot_nt(ones, hi) + _dot_nt(ones, lo)


def _sigmoid(x):
    return 1.0 / (1.0 + jnp.exp(-x))


def _layer_norm(x, g, b):
    mu = jnp.mean(x, axis=-1, keepdims=True)
    xc = x - mu
    var = jnp.mean(xc * xc, axis=-1, keepdims=True)
    return xc * lax.rsqrt(var + LN_EPS) * g + b


def _rotary_t(zt, cos, sin):
    half = ROT_DIM // 2
    parts = []
    for h in range(N_HEADS):
        r0 = h * HEAD_DIM
        x1 = zt[r0:r0 + half]
        x2 = zt[r0 + half:r0 + ROT_DIM]
        parts += [x1 * cos - x2 * sin, x2 * cos + x1 * sin, zt[r0 + ROT_DIM:r0 + HEAD_DIM]]
    return jnp.concatenate(parts, axis=0)


def _cos_sin(inv, pos_row):
    ang = inv * pos_row
    return jnp.cos(ang), jnp.sin(ang)


def _first_argmax(work, idxf, axis, n):
    mx = jnp.max(work, axis=axis, keepdims=True)
    first = jnp.min(jnp.where(work == mx, idxf, float(n)), axis=axis, keepdims=True)
    return mx, first


def _proj_prompt_kernel(x_ref, wn_ref, wt_ref, inv_ref,
                        pooled_ref, qt_ref, k_ref, kb_ref, kmean_ref, v_ref, vt_ref,
                        gp_ref, ga_ref, utail_ref, halo_ref, *, tm):
    i = pl.program_id(1)
    nblk = tm // MOBA_BLOCK

    @pl.when(i == 0)
    def _():
        halo_ref[...] = jnp.zeros_like(halo_ref)

    xb = x_ref[0].astype(BF16)
    zn = _dot(xb, wn_ref[...])
    zt = _dot_nt(wt_ref[...], xb)

    u = zn[:, :POOL_WIDTH]
    ext = jnp.concatenate([halo_ref[...], u], axis=0)
    pos1 = i * tm + lax.broadcasted_iota(jnp.int32, (tm, POOL_GROUP_WIDTH), 0) + 1
    outs = []
    for g, w in enumerate(POOL_WINDOWS):
        c0 = g * POOL_GROUP_WIDTH
        s = ext[:, c0:c0 + POOL_GROUP_WIDTH]
        k = 1
        while k < w:
            n = s.shape[0]
            s = s[k:] + s[:n - k]
            k *= 2
        s = s[HALO_ROWS + 1 - w:HALO_ROWS + 1 - w + tm]
        cnt = jnp.minimum(pos1, w).astype(F32)
        outs.append(s / cnt - u[:, c0:c0 + POOL_GROUP_WIDTH])
    pooled_ref[0] = jnp.concatenate(outs, axis=1).astype(BF16)
    halo_ref[...] = u[tm - HALO_ROWS:]
    utail_ref[0] = u[tm - HALO_ROWS:]

    vt_f = zn[:, POOL_WIDTH:POOL_WIDTH + ATTN_WIDTH].T
    v_ref[0] = vt_f
    vt = vt_f.astype(BF16)
    gates = _sigmoid(zn[:, POOL_WIDTH + ATTN_WIDTH:])
    d = gp_ref.shape[-1]
    gp_ref[0] = gates[:, :d]
    ga_ref[0] = gates[:, d:]

    pos = (i * tm + lax.broadcasted_iota(jnp.int32, (1, tm), 1)).astype(F32)
    cos, sin = _cos_sin(inv_ref[...], pos)
    qt = _rotary_t(zt[:ATTN_WIDTH], cos, sin) * (HEAD_DIM ** -0.5 * LOG2_E)
    kt_f = _rotary_t(zt[ATTN_WIDTH:], cos, sin)
    k_ref[0] = kt_f
    kn = kt_f.T
    kb_ref[0] = kn.astype(BF16)
    for j in range(nblk):
        sl = slice(j * MOBA_BLOCK, (j + 1) * MOBA_BLOCK)
        qt_ref[0, j] = qt[:, sl]
        ones = jnp.ones((V_AUG - HEAD_DIM, MOBA_BLOCK), BF16)
        vt_ref[0, j] = jnp.concatenate(
            [piece for h in range(N_HEADS)
             for piece in (vt[h * HEAD_DIM:(h + 1) * HEAD_DIM, sl], ones)], axis=0)
        kmean_ref[0, pl.ds(i * nblk + j, 1), :] = (
            jnp.sum(kn[sl], axis=0, keepdims=True) * (1.0 / MOBA_BLOCK))


def _proj_prompt(x, wn, wt, inv, tm):
    n, s, d = x.shape
    nb = s // MOBA_BLOCK
    nblk = tm // MOBA_BLOCK
    tok = lambda c, dt: jax.ShapeDtypeStruct((n, s, c), dt)
    blk = lambda dt: jax.ShapeDtypeStruct((n, nb, ATTN_WIDTH, MOBA_BLOCK), dt)
    tok_spec = lambda c: pl.BlockSpec((1, tm, c), lambda b, i: (b, i, 0))
    blk_spec = pl.BlockSpec((1, nblk, ATTN_WIDTH, MOBA_BLOCK), lambda b, i: (b, i, 0, 0))
    full = lambda a: pl.BlockSpec(a.shape, lambda b, i: (0,) * a.ndim)
    tr = jax.ShapeDtypeStruct((n, ATTN_WIDTH, s), F32)
    tr_spec = pl.BlockSpec((1, ATTN_WIDTH, tm), lambda b, i: (b, 0, i))
    return pl.pallas_call(
        functools.partial(_proj_prompt_kernel, tm=tm),
        grid=(n, s // tm),
        in_specs=[tok_spec(d), full(wn), full(wt), full(inv)],
        out_specs=[tok_spec(POOL_WIDTH), blk_spec, tr_spec, tok_spec(ATTN_WIDTH),
                   pl.BlockSpec((1, nb, ATTN_WIDTH), lambda b, i: (b, 0, 0)),
                   tr_spec,
                   pl.BlockSpec((1, nblk, N_HEADS * V_AUG, MOBA_BLOCK), lambda b, i: (b, i, 0, 0)),
                   tok_spec(d), tok_spec(d),
                   pl.BlockSpec((1, HALO_ROWS, POOL_WIDTH), lambda b, i: (b, 0, 0))],
        out_shape=[tok(POOL_WIDTH, BF16), blk(F32), tr, tok(ATTN_WIDTH, BF16),
                   jax.ShapeDtypeStruct((n, nb, ATTN_WIDTH), F32),
                   tr, jax.ShapeDtypeStruct((n, nb, N_HEADS * V_AUG, MOBA_BLOCK), BF16),
                   tok(d, F32), tok(d, F32),
                   jax.ShapeDtypeStruct((n, HALO_ROWS, POOL_WIDTH), F32)],
        scratch_shapes=[pltpu.VMEM((HALO_ROWS, POOL_WIDTH), F32)],
        compiler_params=_cparams(("arbitrary", "arbitrary")),
        name="proj_prompt",
    )(x, wn, wt, inv)


def _proj_sample_kernel(x_ref, wn_ref, wt_ref, inv_ref, st_ref,
                        pooled_ref, u_ref, q_ref, k_ref, v_ref, gp_ref, ga_ref, *, past_len):
    x = x_ref[...]
    uv_w = POOL_WIDTH + ATTN_WIDTH
    zn = _dot3(x, wn_ref[:, :uv_w])
    zt = _dot3(wt_ref[...], x, _dot_nt)
    m = x.shape[0]

    u = zn[:, :POOL_WIDTH]
    u_ref[...] = u
    outs = []
    for g, w in enumerate(POOL_WINDOWS):
        c0 = g * POOL_GROUP_WIDTH
        s = u[:, c0:c0 + POOL_GROUP_WIDTH]
        for r in range(POOL_BUF - (w - 1), POOL_BUF):
            s = s + st_ref[r, :, c0:c0 + POOL_GROUP_WIDTH]
        cnt = float(min(past_len + 1, w))
        outs.append(s / cnt - u[:, c0:c0 + POOL_GROUP_WIDTH])
    pooled_ref[...] = jnp.concatenate(outs, axis=1)

    v_ref[...] = zn[:, POOL_WIDTH:]
    gates = _sigmoid(_dot_bf16(x, wn_ref[:, uv_w:]))
    d = gp_ref.shape[-1]
    gp_ref[...] = gates[:, :d]
    ga_ref[...] = gates[:, d:]

    pos = jnp.full((1, m), float(past_len), F32)
    cos, sin = _cos_sin(inv_ref[...], pos)
    q_ref[...] = (_rotary_t(zt[:ATTN_WIDTH], cos, sin) * (HEAD_DIM ** -0.5)).T
    k_ref[...] = _rotary_t(zt[ATTN_WIDTH:], cos, sin).T


def _proj_sample(x, wn, wt, inv, state_t, past_len):
    m, d = x.shape
    o = lambda c: jax.ShapeDtypeStruct((m, c), F32)
    return pl.pallas_call(
        functools.partial(_proj_sample_kernel, past_len=past_len),
        out_shape=[o(POOL_WIDTH), o(POOL_WIDTH), o(ATTN_WIDTH), o(ATTN_WIDTH),
                   o(ATTN_WIDTH), o(d), o(d)],
        compiler_params=pltpu.CompilerParams(vmem_limit_bytes=VMEM_LIMIT),
        name="proj_sample",
    )(x, wn, wt, inv, state_t)


def _moba_prompt_kernel(qt_ref, kb_ref, vt_ref, kmean_ref, out_ref,
                        sel_ref, qbd_ref, m_ref, acc_ref):
    blk = pl.program_id(1)
    nb = kmean_ref.shape[1]
    npair = ATTN_WIDTH // PAIR
    two = 2 * MOBA_BLOCK
    rows = lax.broadcasted_iota(jnp.int32, (PAIR, MOBA_BLOCK), 0)
    jj = lax.broadcasted_iota(jnp.int32, (nb, two), 0)
    jf = jj.astype(F32)
    valid = jj < blk

    for p in range(npair):
        ps = slice(p * PAIR, (p + 1) * PAIR)
        qt = qt_ref[0, 0, ps, :]
        qbd = jnp.concatenate([jnp.where(rows < HEAD_DIM, qt, 0.0),
                               jnp.where(rows >= HEAD_DIM, qt, 0.0)], axis=1)
        qbd_ref[p] = qbd.astype(BF16)
        work = jnp.where(valid, _dot3(kmean_ref[0, :, ps], qbd), NEG_INF)
        sel = jnp.zeros((nb, two), jnp.bool_)
        for _ in range(min(MOBA_TOPK, nb - 1)):
            _, first = _first_argmax(work, jf, 0, nb)
            pick = jf == first
            sel = jnp.logical_or(sel, pick)
            work = jnp.where(pick, -jnp.inf, work)
        sel_ref[p] = jnp.where(jnp.logical_and(sel, valid), 1.0, 0.0)
    for h in range(N_HEADS):
        m_ref[h] = jnp.full((1, MOBA_BLOCK), NEG_INF, F32)
        acc_ref[h] = jnp.zeros((V_AUG, MOBA_BLOCK), F32)

    half = MOBA_BLOCK // 2
    kpos = lax.broadcasted_iota(jnp.int32, (half, MOBA_BLOCK), 0)
    qpos = lax.broadcasted_iota(jnp.int32, (half, MOBA_BLOCK), 1)

    def attend(j, past):
        row0 = pl.multiple_of(j * MOBA_BLOCK, MOBA_BLOCK)

        def scores(h):
            p, a = divmod(h, 2)
            out = []
            for kh in range(2):
                kt = kb_ref[0, pl.ds(row0 + kh * half, half), p * PAIR:(p + 1) * PAIR]
                st = _dot(kt, qbd_ref[p, :, a * MOBA_BLOCK:(a + 1) * MOBA_BLOCK])
                if not past:
                    st = jnp.where(kpos + kh * half <= qpos, st, NEG_INF)
                out.append(st)
            return out

        sts = {h: scores(h) for h in range(min(SCORE_LOOKAHEAD, N_HEADS))}
        for h in range(N_HEADS):
            if h + SCORE_LOOKAHEAD < N_HEADS:
                sts[h + SCORE_LOOKAHEAD] = scores(h + SCORE_LOOKAHEAD)
            p, a = divmod(h, 2)
            qs = slice(a * MOBA_BLOCK, (a + 1) * MOBA_BLOCK)
            st0, st1 = sts.pop(h)
            m = m_ref[h]
            mj = jnp.maximum(jnp.max(st0, axis=0, keepdims=True), jnp.max(st1, axis=0, keepdims=True))
            if past:
                on = sel_ref[p, pl.ds(j, 1), qs]
                m_new = jnp.where(on > 0.0, jnp.maximum(m, mj), m)
                shift = jnp.where(on > 0.0, m_new, mj)
            else:
                m_new = jnp.maximum(m, mj)
                shift = m_new
            pr0 = jnp.exp2(st0 - shift)
            pr1 = jnp.exp2(st1 - shift)
            alpha = jnp.exp2(m - m_new)
            hs = slice(h * V_AUG, (h + 1) * V_AUG)
            o = (_dot(vt_ref[0, j, hs, :half], pr0.astype(BF16))
                 + _dot(vt_ref[0, j, hs, half:], pr1.astype(BF16)))
            if past:
                o = on * o
            m_ref[h] = m_new
            acc_ref[h] = alpha * acc_ref[h] + o

    def past_block(j, carry):
        attend(j, True)
        return carry

    lax.fori_loop(0, blk, past_block, 0)
    attend(blk, False)

    def normalized(h):
        acc = acc_ref[h]
        return acc[:HEAD_DIM] / acc[HEAD_DIM:HEAD_DIM + 1]

    for p in range(npair):
        o = jnp.concatenate([normalized(2 * p), normalized(2 * p + 1)], axis=0)
        out_ref[0, :, p * PAIR:(p + 1) * PAIR] = o.T


def _moba_prompt(qt, kb, vt, kmean):
    n, nb, _, _ = qt.shape
    s = kb.shape[1]
    npair = ATTN_WIDTH // PAIR
    return pl.pallas_call(
        _moba_prompt_kernel,
        grid=(n, nb),
        in_specs=[pl.BlockSpec((1, 1, ATTN_WIDTH, MOBA_BLOCK), lambda b, j: (b, j, 0, 0)),
                  pl.BlockSpec((1, s, ATTN_WIDTH), lambda b, j: (b, 0, 0)),
                  pl.BlockSpec((1, nb, N_HEADS * V_AUG, MOBA_BLOCK), lambda b, j: (b, 0, 0, 0)),
                  pl.BlockSpec((1, nb, ATTN_WIDTH), lambda b, j: (b, 0, 0))],
        out_specs=pl.BlockSpec((1, MOBA_BLOCK, ATTN_WIDTH), lambda b, j: (b, j, 0)),
        out_shape=jax.ShapeDtypeStruct((n, s, ATTN_WIDTH), F32),
        scratch_shapes=[pltpu.VMEM((npair, nb, 2 * MOBA_BLOCK), F32),
                        pltpu.VMEM((npair, PAIR, 2 * MOBA_BLOCK), BF16),
                        pltpu.VMEM((N_HEADS, 1, MOBA_BLOCK), F32),
                        pltpu.VMEM((N_HEADS, V_AUG, MOBA_BLOCK), F32)],
        compiler_params=_cparams(("arbitrary", "arbitrary")),
        name="moba_prompt",
    )(qt, kb, vt, kmean)


def _sublayer1_kernel(x_ref, pooled_ref, attn_ref, gp_ref, ga_ref, wg_ref, scale_ref, wpp_ref, wap_ref,
                      wout_ref, g1_ref, b1_ref, wr_ref, br_ref, h_ref, w4_ref, gsel_ref, *, precise):
    tm = x_ref.shape[0]
    mm = _dot3 if precise else _dot_bf16
    pooled = pooled_ref[...]
    py = jnp.concatenate(
        [mm(pooled[:, g * POOL_GROUP_WIDTH:(g + 1) * POOL_GROUP_WIDTH], wg_ref[g])
         for g in range(len(POOL_WINDOWS))], axis=1) * scale_ref[...]
    a = mm(py, wpp_ref[...])
    b = mm(attn_ref[...], wap_ref[...])
    merged = gp_ref[...] * a + ga_ref[...] * b
    h = _layer_norm(ALPHA * x_ref[...] + mm(merged, wout_ref[...]), g1_ref[...], b1_ref[...])
    h_ref[...] = h

    lt = (_dot3(h, wr_ref[...]) + br_ref[...]).T
    idx = lax.broadcasted_iota(jnp.int32, (SUBLANES, tm), 0)
    idf = idx.astype(F32)
    gl = jnp.where(idx < N_GROUPS, lt[:SUBLANES], -jnp.inf)
    gmax, gsel = _first_argmax(gl, idf, 0, SUBLANES)
    gsel_ref[0] = gsel
    gexp = jnp.exp(gl - gmax)
    gprob = gexp / jnp.sum(gexp, axis=0, keepdims=True)
    pg = jnp.sum(jnp.where(idf == gsel, gprob, 0.0), axis=0, keepdims=True)
    elog = jnp.zeros((SUBLANES, tm), F32)
    for g in range(N_GROUPS):
        r0 = EXPERT_ROW0 + g * EXPERTS_PER_GROUP
        elog = jnp.where(gsel == float(g), lt[r0:r0 + EXPERTS_PER_GROUP], elog)
    e1, i1 = _first_argmax(elog, idf, 0, SUBLANES)
    e2, i2 = _first_argmax(jnp.where(idf == i1, -jnp.inf, elog), idf, 0, SUBLANES)
    x2 = jnp.exp(e2 - e1)
    den = 1.0 + x2
    w_e = jnp.where(idf == i1, 1.0 / den, 0.0) + jnp.where(idf == i2, x2 / den, 0.0)
    pw = pg * w_e
    slabs = [jnp.where(gsel == float(g), pw, 0.0) for g in range(N_GROUPS)]
    slabs.append(jnp.zeros((ROUTER_LANES - N_GROUPS * EXPERTS_PER_GROUP, tm), F32))
    wn = jnp.concatenate(slabs, axis=0).T
    for g in range(N_GROUPS):
        sh = (ROUTER_LANES - g * EXPERTS_PER_GROUP) % ROUTER_LANES
        w4_ref[g] = wn if sh == 0 else pltpu.roll(wn, sh, 1)


def _sublayer1(x, pooled, attn, gp, ga, wts, tm, precise):
    t, d = x.shape
    row = lambda c: pl.BlockSpec((tm, c), lambda i: (i, 0))
    full = lambda a: pl.BlockSpec(a.shape, lambda i: (0,) * a.ndim)
    return pl.pallas_call(
        functools.partial(_sublayer1_kernel, precise=precise),
        grid=(t // tm,),
        in_specs=[row(d), row(POOL_WIDTH), row(ATTN_WIDTH), row(d), row(d)] + [full(a) for a in wts],
        out_specs=[row(d), pl.BlockSpec((N_GROUPS, tm, ROUTER_LANES), lambda i: (0, i, 0)),
                   pl.BlockSpec((1, 1, tm), lambda i: (i, 0, 0))],
        out_shape=[jax.ShapeDtypeStruct((t, d), F32),
                   jax.ShapeDtypeStruct((N_GROUPS, t, ROUTER_LANES), F32),
                   jax.ShapeDtypeStruct((t // tm, 1, tm), F32)],
        compiler_params=_cparams(("arbitrary",)),
        name="sublayer1",
    )(x, pooled, attn, gp, ga, *wts)


def _moe_kernel(h_ref, w_ref, weg_ref, weu_ref, wed_ref, g2_ref, b2_ref, y_ref, acc_ref):
    g = pl.program_id(1)

    @pl.when(g == 0)
    def _():
        acc_ref[...] = jnp.zeros_like(acc_ref)

    h = h_ref[...]
    hb = h.astype(BF16)
    w = w_ref[0]
    hid = []
    for e in range(EXPERTS_PER_GROUP):
        a = _dot(hb, weg_ref[0, e])
        b = _dot(hb, weu_ref[0, e])
        hid.append((a * _sigmoid(a) * b * w[:, e:e + 1]).astype(BF16))
    acc_ref[...] += _dot(jnp.concatenate(hid, axis=1), wed_ref[0])

    @pl.when(g == pl.num_programs(1) - 1)
    def _():
        y_ref[...] = _layer_norm(ALPHA * h + acc_ref[...], g2_ref[...], b2_ref[...])


def _moe(h, w4, weg, weu, wed, g2, b2, tm):
    t, d = h.shape
    ng, ne, _, f = weg.shape
    return pl.pallas_call(
        _moe_kernel,
        grid=(t // tm, ng),
        in_specs=[pl.BlockSpec((tm, d), lambda i, g: (i, 0)),
                  pl.BlockSpec((1, tm, ROUTER_LANES), lambda i, g: (g, i, 0)),
                  pl.BlockSpec((1, ne, d, f), lambda i, g: (g, 0, 0, 0)),
                  pl.BlockSpec((1, ne, d, f), lambda i, g: (g, 0, 0, 0)),
                  pl.BlockSpec((1, ne * f, d), lambda i, g: (g, 0, 0)),
                  pl.BlockSpec(g2.shape, lambda i, g: (0, 0)),
                  pl.BlockSpec(b2.shape, lambda i, g: (0, 0))],
        out_specs=pl.BlockSpec((tm, d), lambda i, g: (i, 0)),
        out_shape=jax.ShapeDtypeStruct((t, d), F32),
        scratch_shapes=[pltpu.VMEM((tm, d), F32)],
        compiler_params=_cparams(("arbitrary", "arbitrary")),
        name="moe",
    )(h, w4, weg, weu, wed, g2, b2)


MOE_CAP = 320


def _lane_prefix_sum(x):
    n = x.shape[1]
    lane = lax.broadcasted_iota(jnp.int32, x.shape, 1)
    s = 1
    while s < n:
        x = x + jnp.where(lane >= s, pltpu.roll(x, s, 1), 0.0)
        s *= 2
    return x


def _moe_grouped_kernel(h_ref, gsel_ref, w_ref, weg_ref, weu_ref, wed_ref, g2_ref, b2_ref, y_ref,
                        hb_ref, acc_ref, rank_ref, rankt_ref):
    g = pl.program_id(1)
    tm = h_ref.shape[0]
    cap = MOE_CAP

    @pl.when(g == 0)
    def _():
        hb_ref[...] = h_ref[...].astype(BF16)
        acc_ref[...] = jnp.zeros_like(acc_ref)
        sub = lax.broadcasted_iota(jnp.int32, (SUBLANES, tm), 0).astype(F32)
        member = jnp.where(sub == gsel_ref[0], 1.0, 0.0)
        rank = jnp.where(member > 0.0, _lane_prefix_sum(member) - 1.0, -1.0)
        rank_ref[...] = rank
        rank_t = jnp.concatenate([rank, jnp.full((LANES - SUBLANES, tm), -1.0, F32)], axis=0).T
        for gg in range(N_GROUPS):
            rankt_ref[gg] = jnp.broadcast_to(rank_t[:, gg:gg + 1], (tm, LANES))

    rrow = rank_ref[pl.ds(g, 1), :]
    rcol = rankt_ref[g]
    wh, wl = _split_bf16(w_ref[0])
    count = jnp.max(rrow).astype(jnp.int32) + 1
    n_pass = (count + (cap - 1)) // cap
    sub_i = lax.broadcasted_iota(jnp.int32, (cap, tm), 0).astype(F32)
    lane_i = lax.broadcasted_iota(jnp.int32, (tm, LANES), 1).astype(F32)

    def one_pass(ps, carry):
        base = (ps * cap).astype(F32)
        pick = jnp.where(rrow - base == sub_i, 1.0, 0.0).astype(BF16)
        xg = _dot(pick, hb_ref[...]).astype(BF16)
        wg = _dot(pick, wh) + _dot(pick, wl)
        hid = []
        for e in range(EXPERTS_PER_GROUP):
            a = _dot(xg, weg_ref[0, e])
            b = _dot(xg, weu_ref[0, e])
            hid.append((a * _sigmoid(a) * b * wg[:, e:e + 1]).astype(BF16))
        yg = _dot(jnp.concatenate(hid, axis=1), wed_ref[0]).astype(BF16)
        cap_pad = -(-cap // LANES) * LANES
        if cap_pad > cap:
            yg = jnp.concatenate([yg, jnp.zeros((cap_pad - cap, yg.shape[1]), BF16)], axis=0)
        put = jnp.concatenate(
            [jnp.where(rcol - base == lane_i + float(k * LANES), 1.0, 0.0) for k in range(cap_pad // LANES)],
            axis=1).astype(BF16)
        acc_ref[...] += _dot(put, yg)
        return carry

    lax.fori_loop(0, n_pass, one_pass, 0)

    @pl.when(g == pl.num_programs(1) - 1)
    def _():
        y_ref[...] = _layer_norm(ALPHA * h_ref[...] + acc_ref[...], g2_ref[...], b2_ref[...])


def _moe_grouped(h, gsel, w4, weg, weu, wed, g2, b2, tm):
    t, d = h.shape
    ng, ne, _, f = weg.shape
    gsel = gsel.reshape(t // tm, 1, tm)
    return pl.pallas_call(
        _moe_grouped_kernel,
        grid=(t // tm, ng),
        in_specs=[pl.BlockSpec((tm, d), lambda i, g: (i, 0)),
                  pl.BlockSpec((1, 1, tm), lambda i, g: (i, 0, 0)),
                  pl.BlockSpec((1, tm, ROUTER_LANES), lambda i, g: (g, i, 0)),
                  pl.BlockSpec((1, ne, d, f), lambda i, g: (g, 0, 0, 0)),
                  pl.BlockSpec((1, ne, d, f), lambda i, g: (g, 0, 0, 0)),
                  pl.BlockSpec((1, ne * f, d), lambda i, g: (g, 0, 0)),
                  pl.BlockSpec(g2.shape, lambda i, g: (0, 0)),
                  pl.BlockSpec(b2.shape, lambda i, g: (0, 0))],
        out_specs=pl.BlockSpec((tm, d), lambda i, g: (i, 0)),
        out_shape=jax.ShapeDtypeStruct((t, d), F32),
        scratch_shapes=[pltpu.VMEM((tm, d), BF16), pltpu.VMEM((tm, d), F32),
                        pltpu.VMEM((SUBLANES, tm), F32), pltpu.VMEM((N_GROUPS, tm, LANES), F32)],
        compiler_params=_cparams(("arbitrary", "arbitrary")),
        name="moe_grouped",
    )(h, gsel, w4, weg, weu, wed, g2, b2)


def _lane_rep(row):
    return jnp.broadcast_to(row, (LANES, ATTN_WIDTH)).T.reshape(N_HEADS, HEAD_DIM, LANES)


SC_LANES = 16
SC_WORKERS = 32
SC_ROWS = 32
SC_CHUNK = 8


def _sc_logits(cache_rows, row_idx, q_rep, nseq, n_pages):
    per_w = nseq // SC_WORKERS
    chunks = n_pages * 2
    heads_per_chunk = N_HEADS // 2
    rows_per_head = HEAD_DIM // SC_ROWS
    qrows = N_HEADS * HEAD_DIM // SUBLANES
    mesh = plsc.VectorSubcoreMesh(core_axis_name="c", subcore_axis_name="s")

    @functools.partial(
        pl.kernel, mesh=mesh,
        out_type=jax.ShapeDtypeStruct((nseq * n_pages, N_HEADS, PAGE_SIZE), F32),
        scratch_types=[pltpu.VMEM((chunks * SC_CHUNK,), jnp.int32),
                       pltpu.VMEM((qrows, LANES), F32),
                       pltpu.VMEM((2, SC_CHUNK, SC_ROWS, PAGE_SIZE), F32),
                       pltpu.VMEM((N_HEADS, PAGE_SIZE), F32),
                       pltpu.SemaphoreType.DMA((2,))],
    )
    def k(ck_hbm, idx_hbm, q_hbm, out_hbm, idx_v, q_v, kbuf, obuf, sem):
        wid = lax.axis_index("s") * 2 + lax.axis_index("c")

        def chunk_copy(ci, b):
            return pltpu.make_async_copy(ck_hbm.at[idx_v.at[pl.ds(ci * SC_CHUNK, SC_CHUNK)]],
                                         kbuf.at[b], sem.at[b])

        @pl.loop(0, per_w)
        def _(t):
            n = wid * per_w + t
            pltpu.sync_copy(idx_hbm.at[pl.ds(n * (chunks * SC_CHUNK), chunks * SC_CHUNK)], idx_v)
            pltpu.sync_copy(q_hbm.at[n], q_v)
            chunk_copy(0, 0).start()

            @pl.loop(0, n_pages)
            def _(i):
                for g in range(2):
                    ci = i * 2 + g

                    @pl.when(ci + 1 < chunks)
                    def _():
                        chunk_copy(ci + 1, 1 - g).start()

                    chunk_copy(ci, g).wait()
                    @pl.loop(0, heads_per_chunk)
                    def _(hl):
                        h = g * heads_per_chunk + hl
                        accs = [jnp.zeros((SC_LANES,), F32) for _ in range(PAGE_SIZE // SC_LANES)]
                        for r in range(rows_per_head):
                            for dd in range(SC_ROWS):
                                d8, j = divmod(dd, SUBLANES)
                                qv = q_v[h * (HEAD_DIM // SUBLANES) + r * (SC_ROWS // SUBLANES) + d8,
                                         pl.ds(j * SC_LANES, SC_LANES)]
                                for c in range(PAGE_SIZE // SC_LANES):
                                    kv = kbuf[g, hl * rows_per_head + r, dd, pl.ds(c * SC_LANES, SC_LANES)]
                                    accs[c] = accs[c] + qv * kv
                        for c in range(PAGE_SIZE // SC_LANES):
                            obuf[h, pl.ds(c * SC_LANES, SC_LANES)] = accs[c]
                pltpu.sync_copy(obuf, out_hbm.at[n * n_pages + i])

    return k(cache_rows, row_idx, q_rep)


SELECT_TILE = 8


def _sample_select_kernel(lg_ref, q_ref, kn_ref, p_ref, idx_ref, own_ref, *, k_eff):
    ppb = MOBA_BLOCK // PAGE_SIZE
    for s in range(lg_ref.shape[0]):
        x = lg_ref[s]
        nblk = x.shape[0] // ppb
        x = x.reshape(nblk, ppb, N_HEADS, PAGE_SIZE)
        logits = jnp.concatenate([x[:, j] for j in range(ppb)], axis=-1)
        work = jnp.sum(logits, axis=2, keepdims=True) * (1.0 / MOBA_BLOCK)
        bf = lax.broadcasted_iota(jnp.int32, work.shape, 0).astype(F32)
        sel = jnp.zeros(work.shape, jnp.bool_)
        for k in range(k_eff):
            _, first = _first_argmax(work, bf, 0, nblk)
            idx_ref[s, k] = jnp.broadcast_to(first[0], (N_HEADS, LANES)).astype(jnp.int32)
            pick = bf == first
            sel = jnp.logical_or(sel, pick)
            work = jnp.where(pick, -jnp.inf, work)

        s_own = jnp.sum(_lane_rep(kn_ref[s]) * _lane_rep(q_ref[s]), axis=1)[None, :, :1]
        lm = jnp.where(sel, logits, NEG_INF)
        m = jnp.maximum(jnp.max(jnp.max(lm, axis=2, keepdims=True), axis=0, keepdims=True), s_own)
        p = jnp.exp(lm - m)
        p_own = jnp.exp(s_own - m)
        l = jnp.sum(jnp.sum(p, axis=2, keepdims=True), axis=0, keepdims=True) + p_own
        p_ref[s] = p / l
        own_ref[s] = jnp.broadcast_to((p_own / l)[0], (N_HEADS, LANES))


def _sample_select(logits, q3, kn3, k_eff):
    nseq, n_pages = logits.shape[:2]
    nblk = n_pages * PAGE_SIZE // MOBA_BLOCK
    ts = SELECT_TILE if nseq % SELECT_TILE == 0 else 1
    row_spec = pl.BlockSpec((ts, 1, ATTN_WIDTH), lambda n: (n, 0, 0))
    return pl.pallas_call(
        functools.partial(_sample_select_kernel, k_eff=k_eff),
        grid=(nseq // ts,),
        in_specs=[pl.BlockSpec((ts, n_pages, N_HEADS, PAGE_SIZE), lambda n: (n, 0, 0, 0)),
                  row_spec, row_spec],
        out_specs=[pl.BlockSpec((ts, nblk, N_HEADS, MOBA_BLOCK), lambda n: (n, 0, 0, 0)),
                   pl.BlockSpec((ts, k_eff, N_HEADS, LANES), lambda n: (n, 0, 0, 0)),
                   pl.BlockSpec((ts, N_HEADS, LANES), lambda n: (n, 0, 0))],
        out_shape=[jax.ShapeDtypeStruct((nseq, nblk, N_HEADS, MOBA_BLOCK), F32),
                   jax.ShapeDtypeStruct((nseq, k_eff, N_HEADS, LANES), jnp.int32),
                   jax.ShapeDtypeStruct((nseq, N_HEADS, LANES), F32)],
        compiler_params=_cparams(("arbitrary",)),
        name="sample_select",
    )(logits, q3, kn3)


PV_AHEAD = 2
def _sample_pv_kernel(pt_ref, idx_ref, p_ref, own_ref, vn_ref, cv_ref, out_ref,
                      vbuf_ref, sem_ref, *, k_eff):
    n = pl.program_id(0)
    nseq = pl.num_programs(0)
    ppb = MOBA_BLOCK // PAGE_SIZE

    def copies(seq, slot):
        out = []
        for h in range(N_HEADS):
            for k in range(k_eff):
                blk = idx_ref[seq, k, h]
                for j in range(ppb):
                    page = pt_ref[seq, blk * ppb + j]
                    out.append(pltpu.make_async_copy(
                        cv_ref.at[page, h], vbuf_ref.at[slot, h, k * ppb + j], sem_ref.at[slot]))
        return out

    ring = vbuf_ref.shape[0]
    slot = n % ring

    @pl.when(n == 0)
    def _():
        for s in range(PV_AHEAD):
            @pl.when(s < nseq)
            def _():
                for c in copies(s, s):
                    c.start()

    @pl.when(n + PV_AHEAD < nseq)
    def _():
        for c in copies(n + PV_AHEAD, (n + PV_AHEAD) % ring):
            c.start()

    for c in copies(n, slot):
        c.wait()

    accs = []
    for h in range(N_HEADS):
        acc = jnp.zeros((HEAD_DIM, PAGE_SIZE), F32)
        for k in range(k_eff):
            pk = p_ref[0, idx_ref[n, k, h], h:h + 1, :]
            for j in range(ppb):
                acc = acc + pk[:, j * PAGE_SIZE:(j + 1) * PAGE_SIZE] * vbuf_ref[slot, h, k * ppb + j]
        accs.append(acc)
    own = own_ref[0]
    accs += [jnp.broadcast_to(own[h:h + 1], (HEAD_DIM, LANES)) for h in range(N_HEADS)]
    r = _lane_sums_as_rows(jnp.concatenate(accs, axis=0))
    out_ref[0] = r[0:1, :ATTN_WIDTH] + r[0:1, ATTN_WIDTH:] * (1.0 / LANES) * vn_ref[0]


def _sample_pv(page_table, idx, p, own, vn3, cache_vt, k_eff):
    nseq = vn3.shape[0]
    nblk = p.shape[1]
    ppb = MOBA_BLOCK // PAGE_SIZE
    row_spec = pl.BlockSpec((1, 1, ATTN_WIDTH), lambda n, pt, ix: (n, 0, 0))
    return pl.pallas_call(
        functools.partial(_sample_pv_kernel, k_eff=k_eff),
        grid_spec=pltpu.PrefetchScalarGridSpec(
            num_scalar_prefetch=2,
            grid=(nseq,),
            in_specs=[pl.BlockSpec((1, nblk, N_HEADS, MOBA_BLOCK), lambda n, pt, ix: (n, 0, 0, 0)),
                      pl.BlockSpec((1, N_HEADS, LANES), lambda n, pt, ix: (n, 0, 0)),
                      row_spec, pl.BlockSpec(memory_space=pl.ANY)],
            out_specs=row_spec,
            scratch_shapes=[pltpu.VMEM((PV_AHEAD + 1, N_HEADS, k_eff * ppb, HEAD_DIM, PAGE_SIZE), F32),
                            pltpu.SemaphoreType.DMA((PV_AHEAD + 1,))]),
        out_shape=jax.ShapeDtypeStruct((nseq, 1, ATTN_WIDTH), F32),
        compiler_params=_cparams(("arbitrary",)),
        name="sample_pv",
    )(page_table, idx, p, own, vn3, cache_vt)


def kernel(x_prompt, x_sample, cache_k, cache_v, state_pool, page_table, w_in, w_pool_group,
           pool_scale, w_pool_proj, w_attn_proj, w_out, ln1_g, ln1_b, w_router_group,
           b_router_group, w_router_expert, b_router_expert, w_exp_gate, w_exp_up,
           w_exp_down, ln2_g, ln2_b):
    assert w_in.shape[0] == DEPTH == 1
    n, s, d = x_prompt.shape
    nd, ld, _ = x_sample.shape
    assert ld == 1
    past_len = page_table.shape[1] * PAGE_SIZE
    assert past_len % MOBA_BLOCK == 0 and past_len // MOBA_BLOCK >= MOBA_TOPK
    layer = 0

    a0 = POOL_WIDTH
    win = w_in[layer]
    wn_f = jnp.concatenate([win[:, :a0], win[:, a0 + 2 * ATTN_WIDTH:]], axis=1)
    wt_f = win[:, a0:a0 + 2 * ATTN_WIDTH].T
    wn, wt = wn_f.astype(BF16), wt_f.astype(BF16)
    half = ROT_DIM // 2
    inv = (ROPE_THETA ** (-2.0 * jnp.arange(half, dtype=F32) / ROT_DIM)).reshape(half, 1)
    row = lambda a: a.reshape(1, -1)
    ng, ne = N_GROUPS, EXPERTS_PER_GROUP
    wr = jnp.zeros((d, ROUTER_LANES), F32)
    wr = wr.at[:, :ng].set(w_router_group[layer])
    wr = wr.at[:, EXPERT_ROW0:EXPERT_ROW0 + ng * ne].set(
        jnp.transpose(w_router_expert[layer], (1, 0, 2)).reshape(d, ng * ne))
    br = jnp.zeros((1, ROUTER_LANES), F32)
    br = br.at[0, :ng].set(b_router_group[layer])
    br = br.at[0, EXPERT_ROW0:EXPERT_ROW0 + ng * ne].set(b_router_expert[layer].reshape(-1))
    sub_f = (w_pool_group[layer], row(pool_scale[layer]), w_pool_proj[layer], w_attn_proj[layer],
             w_out[layer], row(ln1_g[layer]), row(ln1_b[layer]), wr, br)
    cast = (0, 2, 3, 4)
    sub_b = tuple(a.astype(BF16) if i in cast else a for i, a in enumerate(sub_f))
    weg = w_exp_gate[layer].astype(BF16)
    weu = w_exp_up[layer].astype(BF16)
    f = weg.shape[-1]
    wed = w_exp_down[layer].astype(BF16).reshape(ng, ne * f, d)
    g2, b2 = row(ln2_g[layer]), row(ln2_b[layer])

    tm = 512
    pooled, qt, k_p, kb, kmean, v_p, vt, gp, ga, utail = _proj_prompt(x_prompt, wn, wt, inv, tm)
    attn = _moba_prompt(qt, kb, vt, kmean)
    t = n * s
    flat = lambda a: a.reshape(t, a.shape[-1])
    h_p, w4_p, gsel_p = _sublayer1(flat(x_prompt), flat(pooled), flat(attn), flat(gp), flat(ga),
                                   sub_b, tm, False)
    y_p = _moe_grouped(h_p, gsel_p, w4_p, weg, weu, wed, g2, b2, 2 * tm).reshape(n, s, d)

    xs = x_sample.reshape(nd, d)
    state_t = jnp.transpose(state_pool[layer], (1, 0, 2))
    pooled_s, u_s, q_s, k_s, v_s, gp_s, ga_s = _proj_sample(xs, wn_f, wt_f, inv, state_t, past_len)
    cache_kt = jnp.transpose(cache_k[layer], (0, 2, 3, 1))
    cache_vt = jnp.transpose(cache_v[layer], (0, 2, 3, 1))
    r3 = lambda a: a.reshape(nd, 1, a.shape[-1])
    k_eff = MOBA_TOPK
    assert nd % SC_WORKERS == 0
    n_pages = page_table.shape[1]
    n_pool = cache_k.shape[1]
    cache_rows = cache_kt.reshape(n_pool * 2 * SC_CHUNK, SC_ROWS, PAGE_SIZE)
    row_idx = (page_table[:, :, None] * (2 * SC_CHUNK)
               + jnp.arange(2 * SC_CHUNK, dtype=jnp.int32)[None, None, :]).reshape(-1)
    q_rep = jnp.broadcast_to(q_s.reshape(nd, ATTN_WIDTH // SUBLANES, SUBLANES, 1),
                             (nd, ATTN_WIDTH // SUBLANES, SUBLANES, SC_LANES)
                             ).reshape(nd, ATTN_WIDTH // SUBLANES, LANES)
    logits = _sc_logits(cache_rows, row_idx, q_rep, nd, n_pages).reshape(
        nd, n_pages, N_HEADS, PAGE_SIZE)
    p_sel, idx, own = _sample_select(logits, r3(q_s), r3(k_s), k_eff)
    attn_s = _sample_pv(page_table, jnp.max(idx, axis=-1), p_sel, own, r3(v_s), cache_vt,
                        k_eff).reshape(nd, ATTN_WIDTH)
    h_s, w4_s, _ = _sublayer1(xs, pooled_s, attn_s, gp_s, ga_s, sub_f, nd, True)
    y_s = _moe(h_s, w4_s, weg, weu, wed, g2, b2, nd).reshape(nd, 1, d)

    heads = lambda a, lead: a.reshape(lead + (N_HEADS, HEAD_DIM))
    pool_prompt = utail[:, HALO_ROWS - POOL_BUF:][None]
    pool_sample = jnp.concatenate([state_pool[layer][:, 1:], u_s[:, None, :]], axis=1)[None]
    rows = lambda a: jnp.transpose(a.reshape(n, N_HEADS, HEAD_DIM, s), (0, 3, 1, 2))[None]
    return (y_p, y_s.reshape(nd, ld, d),
            rows(k_p), rows(v_p), pool_prompt,
            heads(k_s, (1, nd, 1)), heads(v_s, (1, nd, 1)), pool_sample)
```
